```python
import math
import jax, jax.numpy as jnp
from jax import lax
import numpy as np

D_MODEL = 1024
BATCH = 2
SEQ = 8192
DEPTH = 2
DEC_BATCH = 1
DEC_SEQ = 16384
PAST_LEN = 128

HEAD_DIM = 64
N_DIFF_HEADS = 4
DIFF_V_DIM = 2 * HEAD_DIM
D_DIFF = N_DIFF_HEADS * DIFF_V_DIM
N_DIL_HEADS = 8
D_DIL = N_DIL_HEADS * HEAD_DIM
D_MIX = D_DIFF + D_DIL
IN_SPLITS = (D_DIFF, D_DIFF, D_DIFF, D_DIL, D_DIL, D_DIL)
D_IN = sum(IN_SPLITS)
D_FF = 4 * D_MODEL
DILATED_PATTERNS = ((128, 1), (512, 4), (2048, 16))
Q_BLOCK = 128
ROPE_THETA = 10000.0
NORM_EPS = 1e-5
NEG_INF = -1e30

kernel_name = 'hymba_diff_dilated_encoder'


def rmsnorm(x, g):
    xf = x.astype(jnp.float32)
    y = xf * lax.rsqrt(jnp.mean(xf * xf, axis=-1, keepdims=True) + NORM_EPS)
    return (y * g.astype(jnp.float32)).astype(x.dtype)


def rope(x, pos):
    half = x.shape[-1] // 2
    inv_freq = ROPE_THETA ** (-jnp.arange(half, dtype=jnp.float32) / half)
    ang = pos.astype(jnp.float32)[:, None] * inv_freq[None, :]
    cos = jnp.cos(ang)[:, None, :]
    sin = jnp.sin(ang)[:, None, :]
    xf = x.astype(jnp.float32)
    x1, x2 = xf[..., :half], xf[..., half:]
    return jnp.concatenate([x1 * cos - x2 * sin, x2 * cos + x1 * sin], axis=-1).astype(x.dtype)


def diff_attention(q1, q2, k1, k2, v, lam):
    B, S, H, dh = q1.shape
    nb = S // Q_BLOCK
    scale = dh ** -0.5
    qq = jnp.stack([q1, q2], axis=0).reshape(2, B, nb, Q_BLOCK, H, dh)
    qq = qq.transpose(2, 0, 1, 3, 4, 5)
    kk = jnp.stack([k1, k2], axis=0)

    def block(qblk):
        s = jnp.einsum('pbqhd,pbkhd->pbhqk', qblk, kk).astype(jnp.float32) * scale
        p = jax.nn.softmax(s, axis=-1)
        a = (p[0] - lam * p[1]).astype(v.dtype)
        return jnp.einsum('bhqk,bkhe->bqhe', a, v)

    o = lax.map(block, qq)
    return o.transpose(1, 0, 2, 3, 4).reshape(B, S, H, v.shape[-1])


def dilated_branch(q, k, v, window, dilation):
    B, S, H, dh = q.shape
    half = window // 2 // dilation
    L = S // dilation
    N = B * dilation

    def by_residue(t):
        return t.reshape(B, L, dilation, H, dh).transpose(0, 2, 1, 3, 4).reshape(N, L, H, dh)

    qs, ks, vs = by_residue(q), by_residue(k), by_residue(v)
    blk = half
    nb = -(-L // blk)
    Lp = nb * blk
    qb = jnp.pad(qs, ((0, 0), (0, Lp - L), (0, 0), (0, 0))).reshape(N, nb, blk, H, dh)
    kv_pad = ((0, 0), (blk, Lp - L + blk), (0, 0), (0, 0))
    kb = jnp.pad(ks, kv_pad).reshape(N, nb + 2, blk, H, dh)
    vb = jnp.pad(vs, kv_pad).reshape(N, nb + 2, blk, H, dh)
    kwin = jnp.concatenate([kb[:, :-2], kb[:, 1:-1], kb[:, 2:]], axis=2)
    vwin = jnp.concatenate([vb[:, :-2], vb[:, 1:-1], vb[:, 2:]], axis=2)
    qi = jnp.arange(Lp).reshape(nb, blk)
    kj = jnp.arange(nb)[:, None] * blk - blk + jnp.arange(3 * blk)[None, :]
    mask = ((jnp.abs(qi[:, :, None] - kj[:, None, :]) <= half)
            & (kj[:, None, :] >= 0) & (kj[:, None, :] < L))
    s = jnp.einsum('nbqhd,nbkhd->nbhqk', qb, kwin).astype(jnp.float32) * (dh ** -0.5)
    s = jnp.where(mask[None, :, None], s, NEG_INF)
    lse = jax.nn.logsumexp(s, axis=-1)
    p = jnp.exp(s - lse[..., None])
    o = jnp.einsum('nbhqk,nbkhd->nbqhd', p.astype(v.dtype), vwin)
    o = o.reshape(N, Lp, H, dh)[:, :L]
    lse = lse.transpose(0, 1, 3, 2).reshape(N, Lp, H)[:, :L]
    o = o.reshape(B, dilation, L, H, dh).transpose(0, 2, 1, 3, 4).reshape(B, S, H, dh)
    lse = lse.reshape(B, dilation, L, H).transpose(0, 2, 1, 3).reshape(B, S, H)
    return o, lse


def dilated_attention(q, k, v):
    outs, lses = [], []
    for window, dilation in DILATED_PATTERNS:
        o, lse = dilated_branch(q, k, v, window, dilation)
        outs.append(o)
        lses.append(lse)
    wts = jax.nn.softmax(jnp.stack(lses, axis=0), axis=0)
    o = jnp.einsum('pbsh,pbshd->bshd', wts, jnp.stack(outs, axis=0).astype(jnp.float32))
    return o.astype(q.dtype)


def encoder_layer(x, layer_idx, norm1_g, w_in, lambda_q1, lambda_k1, lambda_q2, lambda_k2,
                  diff_norm_g, dil_norm_g, w_out, norm2_g, w_ff1, w_ff2):
    B, S, _ = x.shape
    pos = jnp.arange(S)
    h = rmsnorm(x, norm1_g)
    proj = h @ w_in
    dq, dk, dv, sq, sk, sv = jnp.split(proj, list(np.cumsum(IN_SPLITS)[:-1]), axis=-1)

    dq = dq.reshape(B, S, N_DIFF_HEADS, 2, HEAD_DIM)
    dk = dk.reshape(B, S, N_DIFF_HEADS, 2, HEAD_DIM)
    q1, q2 = rope(dq[..., 0, :], pos), rope(dq[..., 1, :], pos)
    k1, k2 = rope(dk[..., 0, :], pos), rope(dk[..., 1, :], pos)
    dv = dv.reshape(B, S, N_DIFF_HEADS, DIFF_V_DIM)
    lambda_init = 0.8 - 0.6 * math.exp(-0.3 * layer_idx)
    lam = (jnp.exp(jnp.sum(lambda_q1.astype(jnp.float32) * lambda_k1.astype(jnp.float32)))
           - jnp.exp(jnp.sum(lambda_q2.astype(jnp.float32) * lambda_k2.astype(jnp.float32)))
           + lambda_init)
    od = diff_attention(q1, q2, k1, k2, dv, lam)
    od = (rmsnorm(od, diff_norm_g) * (1.0 - lambda_init)).reshape(B, S, D_DIFF)

    sq = rope(sq.reshape(B, S, N_DIL_HEADS, HEAD_DIM), pos)
    sk = rope(sk.reshape(B, S, N_DIL_HEADS, HEAD_DIM), pos)
    sv = sv.reshape(B, S, N_DIL_HEADS, HEAD_DIM)
    os_ = rmsnorm(dilated_attention(sq, sk, sv).reshape(B, S, D_DIL), dil_norm_g)

    x = x + jnp.concatenate([od, os_], axis=-1) @ w_out

    h2 = rmsnorm(x, norm2_g)
    return x + jnp.square(jax.nn.relu(h2 @ w_ff1)) @ w_ff2


def trunk(x, norm1_g, w_in, lambda_q1, lambda_k1, lambda_q2, lambda_k2, diff_norm_g,
          dil_norm_g, w_out, norm2_g, w_ff1, w_ff2, final_norm_g):
    for l in range(DEPTH):
        x = encoder_layer(x, l, norm1_g[l], w_in[l], lambda_q1[l], lambda_k1[l], lambda_q2[l],
                          lambda_k2[l], diff_norm_g[l], dil_norm_g[l], w_out[l], norm2_g[l],
                          w_ff1[l], w_ff2[l])
    return rmsnorm(x, final_norm_g)


def setup_inputs(seed: int = 0) -> dict:
    key = jax.random.key(seed)
    ks = jax.random.split(key, 16)
    f32 = jnp.float32
    nrm = lambda k, shape, s: jax.random.normal(k, shape, f32) * s
    return {
        'x_prompt': nrm(ks[0], (BATCH, SEQ, D_MODEL), 1.0),
        'x_sample': nrm(ks[1], (DEC_BATCH, DEC_SEQ, D_MODEL), 1.0),
        'norm1_g': 1.0 + nrm(ks[2], (DEPTH, D_MODEL), 0.01),
        'w_in': nrm(ks[3], (DEPTH, D_MODEL, D_IN), D_MODEL ** -0.5),
        'lambda_q1': nrm(ks[4], (DEPTH, HEAD_DIM), 0.1),
        'lambda_k1': nrm(ks[5], (DEPTH, HEAD_DIM), 0.1),
        'lambda_q2': nrm(ks[6], (DEPTH, HEAD_DIM), 0.1),
        'lambda_k2': nrm(ks[7], (DEPTH, HEAD_DIM), 0.1),
        'diff_norm_g': 1.0 + nrm(ks[8], (DEPTH, DIFF_V_DIM), 0.01),
        'dil_norm_g': 1.0 + nrm(ks[9], (DEPTH, D_DIL), 0.01),
        'w_out': nrm(ks[10], (DEPTH, D_MIX, D_MODEL), D_MIX ** -0.5),
        'norm2_g': 1.0 + nrm(ks[11], (DEPTH, D_MODEL), 0.01),
        'w_ff1': nrm(ks[12], (DEPTH, D_MODEL, D_FF), D_MODEL ** -0.5),
        'w_ff2': nrm(ks[13], (DEPTH, D_FF, D_MODEL), D_FF ** -0.5),
        'final_norm_g': 1.0 + nrm(ks[14], (D_MODEL,), 0.01),
    }


def reference(x_prompt, x_sample, norm1_g, w_in, lambda_q1, lambda_k1, lambda_q2, lambda_k2,
              diff_norm_g, dil_norm_g, w_out, norm2_g, w_ff1, w_ff2, final_norm_g):
    y_prompt = trunk(x_prompt, norm1_g, w_in, lambda_q1, lambda_k1, lambda_q2, lambda_k2,
                     diff_norm_g, dil_norm_g, w_out, norm2_g, w_ff1, w_ff2, final_norm_g)
    y_sample = trunk(x_sample, norm1_g, w_in, lambda_q1, lambda_k1, lambda_q2, lambda_k2,
                     diff_norm_g, dil_norm_g, w_out, norm2_g, w_ff1, w_ff2, final_norm_g)
    return (y_prompt, y_sample)
```

```python
import functools
import math

import jax
import jax.numpy as jnp
from jax import lax
from jax.experimental import pallas as pl
from jax.experimental.pallas import tpu as pltpu

D_MODEL = 1024
HEAD_DIM = 64
N_DIFF_HEADS = 4
DIFF_V_DIM = 2 * HEAD_DIM
D_DIFF = N_DIFF_HEADS * DIFF_V_DIM
N_DIL_HEADS = 8
D_DIL = N_DIL_HEADS * HEAD_DIM
D_IN = 3 * D_DIFF + 3 * D_DIL
D_FF = 4 * D_MODEL
DILATED_PATTERNS = ((128, 1), (512, 4), (2048, 16))
ROPE_THETA = 10000.0
NORM_EPS = 1e-5
NEG_INF = -1e30

LANES = 128
COL_BLOCKS_PER_ROW = D_IN // LANES
DQ_BLK, DK_BLK, DV_BLK = 0, D_DIFF // LANES, 2 * D_DIFF // LANES
SQ_BLK = 3 * D_DIFF // LANES
SK_BLK = SQ_BLK + D_DIL // LANES
SV_BLK = SK_BLK + D_DIL // LANES

VMEM_LIMIT = 56 * 1024 * 1024

BF16 = jnp.bfloat16
F32 = jnp.float32


def _rms(x, g):
    ms = jnp.mean(x * x, axis=-1, keepdims=True)
    return x * lax.rsqrt(ms + NORM_EPS) * g


IN_CHUNK = 512
IN_CHUNK_KIND = ((True, HEAD_DIM ** -0.5), (True, 1.0), (False, 1.0),
                 (True, HEAD_DIM ** -0.5), (True, 1.0), (False, 1.0))


def _in_proj_kernel(x_ref, g_ref, w_ref, cos_ref, sin_ref, o_ref):
    h = _rms(x_ref[...], g_ref[...]).astype(BF16)
    cos = cos_ref[...]
    sin = sin_ref[...]
    lane = lax.broadcasted_iota(jnp.int32, (1, LANES), 1)
    first_half = (lane % HEAD_DIM) < (HEAD_DIM // 2)
    for c, (rope, scale) in enumerate(IN_CHUNK_KIND):
        acc = jnp.dot(h, w_ref[:, c * IN_CHUNK:(c + 1) * IN_CHUNK], preferred_element_type=F32)
        if not rope:
            o_ref[:, c * IN_CHUNK:(c + 1) * IN_CHUNK] = acc.astype(BF16)
            continue
        for g in range(IN_CHUNK // LANES):
            xg = acc[:, g * LANES:(g + 1) * LANES]
            partner = jnp.where(first_half,
                                pltpu.roll(xg, LANES - HEAD_DIM // 2, axis=1),
                                pltpu.roll(xg, HEAD_DIM // 2, axis=1))
            r = xg * cos + partner * sin
            if scale != 1.0:
                r = r * scale
            lo = c * IN_CHUNK + g * LANES
            o_ref[:, lo:lo + LANES] = r.astype(BF16)


def _in_proj(x, g, w, cos, sin, tm):
    t = x.shape[0]
    return pl.pallas_call(
        _in_proj_kernel,
        grid=(t // tm,),
        in_specs=[
            pl.BlockSpec((tm, D_MODEL), lambda i: (i, 0)),
            pl.BlockSpec((1, D_MODEL), lambda i: (0, 0)),
            pl.BlockSpec((D_MODEL, D_IN), lambda i: (0, 0)),
            pl.BlockSpec((tm, LANES), lambda i: (i, 0)),
            pl.BlockSpec((tm, LANES), lambda i: (i, 0)),
        ],
        out_specs=pl.BlockSpec((tm, D_IN), lambda i: (i, 0)),
        out_shape=jax.ShapeDtypeStruct((t, D_IN), BF16),
        compiler_params=pltpu.CompilerParams(
            dimension_semantics=("arbitrary",), vmem_limit_bytes=VMEM_LIMIT),
        name="in_proj",
    )(x, g, w, cos, sin)


def _diff_attn_kernel(q_ref, k_ref, v_ref, lq1_ref, lk1_ref, lq2_ref, lk2_ref, g_ref, o_ref,
                      m_ref, l_ref, acc_ref, *, tk, lambda_init):
    s_len = k_ref.shape[0]
    q = q_ref[...]
    lane = lax.broadcasted_iota(jnp.int32, (1, LANES), 1)
    zero = jnp.zeros_like(q)
    qm = (jnp.where(lane < HEAD_DIM, q, zero), jnp.where(lane >= HEAD_DIM, q, zero))

    m_ref[...] = jnp.full(m_ref.shape, NEG_INF, F32)
    l_ref[...] = jnp.zeros(l_ref.shape, F32)
    acc_ref[...] = jnp.zeros(acc_ref.shape, F32)

    def body(j, carry):
        start = pl.multiple_of(j * tk, tk)
        kc = k_ref[pl.ds(start, tk), :]
        vc = v_ref[pl.ds(start, tk), :]
        for p in range(2):
            s = lax.dot_general(qm[p], kc, (((1,), (1,)), ((), ())), preferred_element_type=F32)
            m_old = m_ref[p]
            m_new = jnp.maximum(m_old, jnp.max(s, axis=-1, keepdims=True))
            alpha = jnp.exp(m_old - m_new)
            e = jnp.exp(s - m_new)
            l_ref[p] = alpha * l_ref[p] + jnp.sum(e, axis=-1, keepdims=True)
            acc_ref[p] = alpha * acc_ref[p] + jnp.dot(e.astype(BF16), vc, preferred_element_type=F32)
            m_ref[p] = m_new
        return carry

    lax.fori_loop(0, s_len // tk, body, 0)

    lam = (jnp.exp(jnp.sum(lq1_ref[...] * lk1_ref[...], axis=-1, keepdims=True))
           - jnp.exp(jnp.sum(lq2_ref[...] * lk2_ref[...], axis=-1, keepdims=True))
           + lambda_init)
    o = acc_ref[0] / l_ref[0] - lam * (acc_ref[1] / l_ref[1])
    o = _rms(o, g_ref[...]) * (1.0 - lambda_init)
    o_ref[...] = o.astype(o_ref.dtype)


def _diff_attn(proj, lq1, lk1, lq2, lk2, g, batch, s_len, lambda_init, tq, tk):
    assert s_len % tq == 0 and s_len % tk == 0
    nq = s_len // tq
    small = pl.BlockSpec((1, HEAD_DIM), lambda b, h, i: (0, 0))
    return pl.pallas_call(
        functools.partial(_diff_attn_kernel, tk=tk, lambda_init=lambda_init),
        grid=(batch, N_DIFF_HEADS, nq),
        in_specs=[
            pl.BlockSpec((tq, LANES), lambda b, h, i: (b * nq + i, DQ_BLK + h)),
            pl.BlockSpec((s_len, LANES), lambda b, h, i: (b, DK_BLK + h)),
            pl.BlockSpec((s_len, LANES), lambda b, h, i: (b, DV_BLK + h)),
            small, small, small, small,
            pl.BlockSpec((1, DIFF_V_DIM), lambda b, h, i: (0, 0)),
        ],
        out_specs=pl.BlockSpec((tq, LANES), lambda b, h, i: (b * nq + i, h)),
        out_shape=jax.ShapeDtypeStruct((batch * s_len, D_DIFF), BF16),
        scratch_shapes=[
            pltpu.VMEM((2, tq, 1), F32),
            pltpu.VMEM((2, tq, 1), F32),
            pltpu.VMEM((2, tq, DIFF_V_DIM), F32),
        ],
        compiler_params=pltpu.CompilerParams(
            dimension_semantics=("arbitrary", "arbitrary", "arbitrary"),
            vmem_limit_bytes=VMEM_LIMIT),
        name="diff_attn",
    )(proj, proj, proj, lq1, lk1, lq2, lk2, g)


DIL_HALF = 64
DIL_TQ = 256


def _dilated_kernel(q_ref, k_ref, v_ref, o_ref, lse_ref, *, win):
    l_len = k_ref.shape[0]
    tq = q_ref.shape[0]
    q0 = pl.program_id(3) * tq
    start = jnp.clip(q0 - DIL_HALF, 0, l_len - win)
    start = pl.multiple_of(start, DIL_HALF)
    kw = k_ref[pl.ds(start, win), :]
    vw = v_ref[pl.ds(start, win), :]
    q = q_ref[...]
    lane = lax.broadcasted_iota(jnp.int32, (1, LANES), 1)
    qi = q0 + lax.broadcasted_iota(jnp.int32, (tq, win), 0)
    kj = start + lax.broadcasted_iota(jnp.int32, (tq, win), 1)
    band = jnp.abs(qi - kj) <= DIL_HALF
    zero = jnp.zeros_like(q)
    outs, lses = [], []
    for hh in range(2):
        in_head = (lane >= hh * HEAD_DIM) & (lane < (hh + 1) * HEAD_DIM)
        s = lax.dot_general(jnp.where(in_head, q, zero), kw, (((1,), (1,)), ((), ())),
                            preferred_element_type=F32)
        s = jnp.where(band, s, NEG_INF)
        m = jnp.max(s, axis=-1, keepdims=True)
        e = jnp.exp(s - m)
        l = jnp.sum(e, axis=-1, keepdims=True)
        outs.append(jnp.dot(e.astype(BF16), vw, preferred_element_type=F32) / l)
        lses.append(m + jnp.log(l))
    o_ref[...] = jnp.where(lane < HEAD_DIM, outs[0], outs[1])
    lse_ref[...] = jnp.where(lane < HEAD_DIM, lses[0], lses[1])


def _dilated(proj, dil, batch, s_len):
    rows = batch * s_len
    l_len = s_len // dil
    assert s_len % dil == 0 and l_len % DIL_HALF == 0
    if l_len <= 2 * DIL_TQ:
        tq, win = l_len, l_len
    else:
        tq, win = DIL_TQ, DIL_TQ + 2 * DIL_HALF
    assert l_len % tq == 0
    nq = l_len // tq
    view = proj.reshape(rows // dil, dil * D_IN)
    n_pair = D_DIL // LANES
    out_sds = jax.ShapeDtypeStruct((rows // dil, dil * D_DIL), F32)
    out_spec = pl.BlockSpec((tq, LANES), lambda b, r, hp, i: (b * nq + i, r * n_pair + hp))
    o, lse = pl.pallas_call(
        functools.partial(_dilated_kernel, win=win),
        grid=(batch, dil, n_pair, nq),
        in_specs=[
            pl.BlockSpec((tq, LANES),
                         lambda b, r, hp, i: (b * nq + i, r * COL_BLOCKS_PER_ROW + SQ_BLK + hp)),
            pl.BlockSpec((l_len, LANES),
                         lambda b, r, hp, i: (b, r * COL_BLOCKS_PER_ROW + SK_BLK + hp)),
            pl.BlockSpec((l_len, LANES),
                         lambda b, r, hp, i: (b, r * COL_BLOCKS_PER_ROW + SV_BLK + hp)),
        ],
        out_specs=[out_spec, out_spec],
        out_shape=[out_sds, out_sds],
        compiler_params=pltpu.CompilerParams(
            dimension_semantics=("arbitrary",) * 4, vmem_limit_bytes=VMEM_LIMIT),
        name=f"dilated_d{dil}",
    )(view, view, view)
    return o.reshape(rows, D_DIL), lse.reshape(rows, D_DIL)


FF_CHUNK = 512


def _out_ffn_kernel(x_ref, od_ref, o0_ref, l0_ref, o1_ref, l1_ref, o2_ref, l2_ref, gdil_ref,
                    wout_ref, g2_ref, w1_ref, w2_ref, gfin_ref, y_ref, *, final_norm):
    l0, l1, l2 = l0_ref[...], l1_ref[...], l2_ref[...]
    lmax = jnp.maximum(jnp.maximum(l0, l1), l2)
    w0, w1, w2 = jnp.exp(l0 - lmax), jnp.exp(l1 - lmax), jnp.exp(l2 - lmax)
    den = w0 + w1 + w2
    os_ = (w0 * o0_ref[...] + w1 * o1_ref[...] + w2 * o2_ref[...]) / den
    os_ = _rms(os_, gdil_ref[...]).astype(BF16)
    x = x_ref[...]
    x = x + jnp.dot(od_ref[...], wout_ref[:D_DIFF, :], preferred_element_type=F32)
    x = x + jnp.dot(os_, wout_ref[D_DIFF:, :], preferred_element_type=F32)
    h2 = _rms(x, g2_ref[...]).astype(BF16)
    acc = jnp.zeros_like(x)
    for c in range(D_FF // FF_CHUNK):
        a = jnp.dot(h2, w1_ref[:, c * FF_CHUNK:(c + 1) * FF_CHUNK], preferred_element_type=F32)
        a = jnp.square(jnp.maximum(a, 0.0)).astype(BF16)
        acc = acc + jnp.dot(a, w2_ref[c * FF_CHUNK:(c + 1) * FF_CHUNK, :], preferred_element_type=F32)
    y = x + acc
    if final_norm:
        y = _rms(y, gfin_ref[...])
    y_ref[...] = y


def _out_ffn(x, od, dil_parts, gdil, wout, g2, w1, w2, gfin, final_norm, tm):
    t = x.shape[0]
    row = lambda w: pl.BlockSpec((tm, w), lambda i: (i, 0))
    const = lambda a: pl.BlockSpec(a.shape, lambda i: (0, 0), pipeline_mode=pl.Buffered(1))
    flat = [a for pair in dil_parts for a in pair]
    return pl.pallas_call(
        functools.partial(_out_ffn_kernel, final_norm=final_norm),
        grid=(t // tm,),
        in_specs=[row(D_MODEL), row(D_DIFF)] + [row(D_DIL)] * 6
                 + [const(gdil), const(wout), const(g2), const(w1), const(w2), const(gfin)],
        out_specs=row(D_MODEL),
        out_shape=jax.ShapeDtypeStruct((t, D_MODEL), F32),
        compiler_params=pltpu.CompilerParams(
            dimension_semantics=("arbitrary",), vmem_limit_bytes=VMEM_LIMIT),
        name="out_ffn",
    )(x, od, *flat, gdil, wout, g2, w1, w2, gfin)


def _rope_tables(batch, s_len):
    half = HEAD_DIM // 2
    inv_freq = ROPE_THETA ** (-jnp.arange(half, dtype=F32) / half)
    ang = jnp.arange(s_len).astype(F32)[:, None] * inv_freq[None, :]
    cos, sin = jnp.cos(ang), jnp.sin(ang)
    reps = LANES // HEAD_DIM
    cos = jnp.tile(cos, (batch, 2 * reps))
    sin = jnp.tile(jnp.concatenate([-sin, sin], axis=1), (batch, reps))
    return cos, sin


def _trunk(x, params, tm, tq, tk):
    (norm1_g, w_in, lq1, lk1, lq2, lk2, diff_norm_g, dil_norm_g, w_out, norm2_g, w_ff1, w_ff2,
     final_norm_g) = params
    batch, s_len, _ = x.shape
    depth = w_in.shape[0]
    cos, sin = _rope_tables(batch, s_len)
    row2 = lambda a: a.reshape(1, -1)
    y = x.reshape(batch * s_len, D_MODEL)
    for l in range(depth):
        lambda_init = 0.8 - 0.6 * math.exp(-0.3 * l)
        proj = _in_proj(y, row2(norm1_g[l]), w_in[l], cos, sin, min(tm, batch * s_len))
        od = _diff_attn(proj, row2(lq1[l]), row2(lk1[l]), row2(lq2[l]), row2(lk2[l]),
                        row2(diff_norm_g[l]), batch, s_len, lambda_init,
                        min(tq, s_len), min(tk, s_len))
        dil_parts = [_dilated(proj, dil, batch, s_len) for _, dil in DILATED_PATTERNS]
        y = _out_ffn(y, od, dil_parts, row2(dil_norm_g[l]), w_out[l], row2(norm2_g[l]),
                     w_ff1[l], w_ff2[l], row2(final_norm_g), l == depth - 1,
                     min(tm, batch * s_len))
    return y.reshape(x.shape)


def kernel(x_prompt, x_sample, norm1_g, w_in, lambda_q1, lambda_k1, lambda_q2, lambda_k2, diff_norm_g, dil_norm_g, w_out, norm2_g, w_ff1, w_ff2, final_norm_g):
    params = (norm1_g, w_in.astype(BF16), lambda_q1, lambda_k1, lambda_q2, lambda_k2, diff_norm_g,
              dil_norm_g, w_out.astype(BF16), norm2_g, w_ff1.astype(BF16), w_ff2.astype(BF16),
              final_norm_g)
    return (_trunk(x_prompt, params, tm=512, tq=512, tk=512),
            _trunk(x_sample, params, tm=512, tq=512, tk=512))
```

```python
import functools
import math

import jax
import jax.numpy as jnp
from jax import lax
from jax.experimental import pallas as pl
from jax.experimental.pallas import tpu as pltpu

D_MODEL = 1024
HEAD_DIM = 64
N_DIFF_HEADS = 4
DIFF_V_DIM = 2 * HEAD_DIM
D_DIFF = N_DIFF_HEADS * DIFF_V_DIM
N_DIL_HEADS = 8
D_DIL = N_DIL_HEADS * HEAD_DIM
D_IN = 3 * D_DIFF + 3 * D_DIL
D_FF = 4 * D_MODEL
DILATED_PATTERNS = ((128, 1), (512, 4), (2048, 16))
ROPE_THETA = 10000.0
NORM_EPS = 1e-5
NEG_INF = -1e30

LANES = 128
COL_BLOCKS_PER_ROW = D_IN // LANES
DQ_BLK, DK_BLK, DV_BLK = 0, D_DIFF // LANES, 2 * D_DIFF // LANES
SQ_BLK = 3 * D_DIFF // LANES
SK_BLK = SQ_BLK + D_DIL // LANES
SV_BLK = SK_BLK + D_DIL // LANES

VMEM_LIMIT = 56 * 1024 * 1024

BF16 = jnp.bfloat16
F32 = jnp.float32


def _rms(x, g):
    ms = jnp.mean(x * x, axis=-1, keepdims=True)
    return x * lax.rsqrt(ms + NORM_EPS) * g


IN_CHUNK = 512
DQ_SCALE = HEAD_DIM ** -0.5 * math.log2(math.e)
IN_CHUNK_KIND = ((True, DQ_SCALE), (True, 1.0), (False, 1.0),
                 (True, HEAD_DIM ** -0.5), (True, 1.0), (False, 1.0))


def _in_proj_kernel(x_ref, g_ref, w_ref, cos_ref, sin_ref, o_ref):
    h = _rms(x_ref[...], g_ref[...]).astype(BF16)
    cos = cos_ref[...]
    sin = sin_ref[...]
    lane = lax.broadcasted_iota(jnp.int32, (1, LANES), 1)
    first_half = (lane % HEAD_DIM) < (HEAD_DIM // 2)
    for c, (rope, scale) in enumerate(IN_CHUNK_KIND):
        acc = jnp.dot(h, w_ref[:, c * IN_CHUNK:(c + 1) * IN_CHUNK], preferred_element_type=F32)
        if not rope:
            o_ref[:, c * IN_CHUNK:(c + 1) * IN_CHUNK] = acc.astype(BF16)
            continue
        for g in range(IN_CHUNK // LANES):
            xg = acc[:, g * LANES:(g + 1) * LANES]
            partner = jnp.where(first_half,
                                pltpu.roll(xg, LANES - HEAD_DIM // 2, axis=1),
                                pltpu.roll(xg, HEAD_DIM // 2, axis=1))
            r = xg * cos + partner * sin
            if scale != 1.0:
                r = r * scale
            lo = c * IN_CHUNK + g * LANES
            o_ref[:, lo:lo + LANES] = r.astype(BF16)


def _in_proj(x, g, w, cos, sin, tm):
    t = x.shape[0]
    return pl.pallas_call(
        _in_proj_kernel,
        grid=(t // tm,),
        in_specs=[
            pl.BlockSpec((tm, D_MODEL), lambda i: (i, 0)),
            pl.BlockSpec((1, D_MODEL), lambda i: (0, 0)),
            pl.BlockSpec((D_MODEL, D_IN), lambda i: (0, 0)),
            pl.BlockSpec((tm, LANES), lambda i: (i, 0)),
            pl.BlockSpec((tm, LANES), lambda i: (i, 0)),
        ],
        out_specs=pl.BlockSpec((tm, D_IN), lambda i: (i, 0)),
        out_shape=jax.ShapeDtypeStruct((t, D_IN), BF16),
        compiler_params=pltpu.CompilerParams(
            dimension_semantics=("arbitrary",), vmem_limit_bytes=VMEM_LIMIT),
        name="in_proj",
    )(x, g, w, cos, sin)


VT_ROWS = DIFF_V_DIM + 16
DIFF_UNROLL = 4


def _diff_attn_kernel(q_ref, k_ref, v_ref, lq1_ref, lk1_ref, lq2_ref, lk2_ref, g_ref, o_ref,
                      vt_ref, sa_ref, sb_ref, cma_ref, cmb_ref, pa_ref, pb_ref, ala_ref, alb_ref,
                      m_ref, acc_ref, *, lambda_init, unroll):
    n_chunks, _, tk = vt_ref.shape

    @pl.when(pl.program_id(2) == 0)
    def _():
        row = lax.broadcasted_iota(jnp.int32, (VT_ROWS - DIFF_V_DIM, tk), 0)
        ones_tile = jnp.where(row == 0, 1.0, 0.0).astype(BF16)

        def transpose_chunk(c, carry):
            start = pl.multiple_of(c * tk, tk)
            vt_ref[c, :DIFF_V_DIM, :] = v_ref[pl.ds(start, tk), :].astype(F32).T.astype(BF16)
            vt_ref[c, DIFF_V_DIM:, :] = ones_tile
            return carry
        lax.fori_loop(0, n_chunks, transpose_chunk, 0)

    q = q_ref[...]
    lane = lax.broadcasted_iota(jnp.int32, (1, LANES), 1)
    zero = jnp.zeros_like(q)
    qm = (jnp.where(lane < HEAD_DIM, q, zero), jnp.where(lane >= HEAD_DIM, q, zero))

    m_ref[...] = jnp.full(m_ref.shape, NEG_INF, F32)
    acc_ref[...] = jnp.zeros(acc_ref.shape, F32)

    s_bufs, cmax_bufs = (sa_ref, sb_ref), (cma_ref, cmb_ref)
    p_bufs, alpha_bufs = (pa_ref, pb_ref), (ala_ref, alb_ref)

    def scores(j, slot):
        start = pl.multiple_of(j * tk, tk)
        kc = k_ref[pl.ds(start, tk), :]
        for p in range(2):
            s = lax.dot_general(kc, qm[p], (((1,), (1,)), ((), ())),
                                preferred_element_type=F32)
            s_bufs[slot][p] = s
            cmax_bufs[slot][p] = jnp.max(s, axis=0, keepdims=True)

    def softmax(slot):
        for p in range(2):
            m_old = m_ref[p]
            m_new = jnp.maximum(m_old, cmax_bufs[slot][p])
            alpha_bufs[slot][p] = jnp.exp2(m_old - m_new)
            p_bufs[slot][p] = jnp.exp2(s_bufs[slot][p] - m_new).astype(BF16)
            m_ref[p] = m_new

    def pv(j, slot):
        vt = vt_ref[j]
        for p in range(2):
            acc_ref[p] = (alpha_bufs[slot][p] * acc_ref[p]
                          + jnp.dot(vt, p_bufs[slot][p], preferred_element_type=F32))

    scores(0, 0)
    pb_ref[...] = jnp.zeros(pb_ref.shape, BF16)
    alb_ref[...] = jnp.ones(alb_ref.shape, F32)

    def body(jj, carry):
        for u in range(unroll):
            j = unroll * jj + u
            cur, nxt = u % 2, 1 - u % 2
            pv(jnp.maximum(j - 1, 0), nxt)
            softmax(cur)
            scores(jnp.minimum(j + 1, n_chunks - 1), nxt)
        return carry

    lax.fori_loop(0, n_chunks // unroll, body, 0)
    pv(n_chunks - 1, (n_chunks - 1) % 2)

    lam = (jnp.exp(jnp.sum(lq1_ref[...] * lk1_ref[...], axis=-1, keepdims=True))
           - jnp.exp(jnp.sum(lq2_ref[...] * lk2_ref[...], axis=-1, keepdims=True))
           + lambda_init)
    o1 = acc_ref[0, :DIFF_V_DIM, :] / acc_ref[0, DIFF_V_DIM:DIFF_V_DIM + 1, :]
    o2 = acc_ref[1, :DIFF_V_DIM, :] / acc_ref[1, DIFF_V_DIM:DIFF_V_DIM + 1, :]
    o = o1 - lam * o2
    ms = jnp.mean(o * o, axis=0, keepdims=True)
    o = o * lax.rsqrt(ms + NORM_EPS) * g_ref[...] * (1.0 - lambda_init)
    o_ref[...] = o.T.astype(o_ref.dtype)


def _diff_attn(proj, lq1, lk1, lq2, lk2, g, batch, s_len, lambda_init, tq, tk):
    assert s_len % tq == 0 and s_len % (2 * tk) == 0
    nq = s_len // tq
    n_chunks = s_len // tk
    unroll = DIFF_UNROLL if n_chunks % DIFF_UNROLL == 0 else 2
    small = pl.BlockSpec((1, HEAD_DIM), lambda b, h, i: (0, 0))
    return pl.pallas_call(
        functools.partial(_diff_attn_kernel, lambda_init=lambda_init, unroll=unroll),
        grid=(batch, N_DIFF_HEADS, nq),
        in_specs=[
            pl.BlockSpec((tq, LANES), lambda b, h, i: (b * nq + i, DQ_BLK + h)),
            pl.BlockSpec((s_len, LANES), lambda b, h, i: (b, DK_BLK + h)),
            pl.BlockSpec((s_len, LANES), lambda b, h, i: (b, DV_BLK + h)),
            small, small, small, small,
            pl.BlockSpec((DIFF_V_DIM, 1), lambda b, h, i: (0, 0)),
        ],
        out_specs=pl.BlockSpec((tq, LANES), lambda b, h, i: (b * nq + i, h)),
        out_shape=jax.ShapeDtypeStruct((batch * s_len, D_DIFF), BF16),
        scratch_shapes=[
            pltpu.VMEM((s_len // tk, VT_ROWS, tk), BF16),
            pltpu.VMEM((2, tk, tq), F32),
            pltpu.VMEM((2, tk, tq), F32),
            pltpu.VMEM((2, 1, tq), F32),
            pltpu.VMEM((2, 1, tq), F32),
            pltpu.VMEM((2, tk, tq), BF16),
            pltpu.VMEM((2, tk, tq), BF16),
            pltpu.VMEM((2, 1, tq), F32),
            pltpu.VMEM((2, 1, tq), F32),
            pltpu.VMEM((2, 1, tq), F32),
            pltpu.VMEM((2, VT_ROWS, tq), F32),
        ],
        compiler_params=pltpu.CompilerParams(
            dimension_semantics=("arbitrary", "arbitrary", "arbitrary"),
            vmem_limit_bytes=VMEM_LIMIT),
        name="diff_attn",
    )(proj, proj, proj, lq1, lk1, lq2, lk2, g)


DIL_HALF = 64
DIL_TQ = 256


def _dilated_kernel(q_ref, k_ref, v_ref, o_ref, lse_ref, *, win):
    l_len = k_ref.shape[0]
    tq = q_ref.shape[0]
    q0 = pl.program_id(3) * tq
    start = jnp.clip(q0 - DIL_HALF, 0, l_len - win)
    start = pl.multiple_of(start, DIL_HALF)
    kw = k_ref[pl.ds(start, win), :]
    vw = v_ref[pl.ds(start, win), :]
    q = q_ref[...]
    lane = lax.broadcasted_iota(jnp.int32, (1, LANES), 1)
    qi = q0 + lax.broadcasted_iota(jnp.int32, (tq, win), 0)
    kj = start + lax.broadcasted_iota(jnp.int32, (tq, win), 1)
    band = jnp.abs(qi - kj) <= DIL_HALF
    zero = jnp.zeros_like(q)
    outs, lses = [], []
    for hh in range(2):
        in_head = (lane >= hh * HEAD_DIM) & (lane < (hh + 1) * HEAD_DIM)
        s = lax.dot_general(jnp.where(in_head, q, zero), kw, (((1,), (1,)), ((), ())),
                            preferred_element_type=F32)
        s = jnp.where(band, s, NEG_INF)
        m = jnp.max(s, axis=-1, keepdims=True)
        e = jnp.exp(s - m)
        l = jnp.sum(e, axis=-1, keepdims=True)
        outs.append(jnp.dot(e.astype(BF16), vw, preferred_element_type=F32) / l)
        lses.append(m + jnp.log(l))
    o_ref[...] = jnp.where(lane < HEAD_DIM, outs[0], outs[1])
    lse_ref[...] = jnp.where(lane < HEAD_DIM, lses[0], lses[1])


def _dilated(proj, dil, batch, s_len):
    rows = batch * s_len
    l_len = s_len // dil
    assert s_len % dil == 0 and l_len % DIL_HALF == 0
    if l_len <= 2 * DIL_TQ:
        tq, win = l_len, l_len
    else:
        tq, win = DIL_TQ, DIL_TQ + 2 * DIL_HALF
    assert l_len % tq == 0
    nq = l_len // tq
    view = proj.reshape(rows // dil, dil * D_IN)
    n_pair = D_DIL // LANES
    out_sds = jax.ShapeDtypeStruct((rows // dil, dil * D_DIL), F32)
    out_spec = pl.BlockSpec((tq, LANES), lambda b, r, hp, i: (b * nq + i, r * n_pair + hp))
    o, lse = pl.pallas_call(
        functools.partial(_dilated_kernel, win=win),
        grid=(batch, dil, n_pair, nq),
        in_specs=[
            pl.BlockSpec((tq, LANES),
                         lambda b, r, hp, i: (b * nq + i, r * COL_BLOCKS_PER_ROW + SQ_BLK + hp)),
            pl.BlockSpec((l_len, LANES),
                         lambda b, r, hp, i: (b, r * COL_BLOCKS_PER_ROW + SK_BLK + hp)),
            pl.BlockSpec((l_len, LANES),
                         lambda b, r, hp, i: (b, r * COL_BLOCKS_PER_ROW + SV_BLK + hp)),
        ],
        out_specs=[out_spec, out_spec],
        out_shape=[out_sds, out_sds],
        compiler_params=pltpu.CompilerParams(
            dimension_semantics=("arbitrary",) * 4, vmem_limit_bytes=VMEM_LIMIT),
        name=f"dilated_d{dil}",
    )(view, view, view)
    return o.reshape(rows, D_DIL), lse.reshape(rows, D_DIL)


FF_CHUNK = 512


def _out_ffn_kernel(x_ref, od_ref, o0_ref, l0_ref, o1_ref, l1_ref, o2_ref, l2_ref, gdil_ref,
                    wout_ref, g2_ref, w1_ref, w2_ref, gfin_ref, y_ref, *, final_norm):
    l0, l1, l2 = l0_ref[...], l1_ref[...], l2_ref[...]
    lmax = jnp.maximum(jnp.maximum(l0, l1), l2)
    w0, w1, w2 = jnp.exp(l0 - lmax), jnp.exp(l1 - lmax), jnp.exp(l2 - lmax)
    den = w0 + w1 + w2
    os_ = (w0 * o0_ref[...] + w1 * o1_ref[...] + w2 * o2_ref[...]) / den
    os_ = _rms(os_, gdil_ref[...]).astype(BF16)
    x = x_ref[...]
    x = x + jnp.dot(od_ref[...], wout_ref[:D_DIFF, :], preferred_element_type=F32)
    x = x + jnp.dot(os_, wout_ref[D_DIFF:, :], preferred_element_type=F32)
    h2 = _rms(x, g2_ref[...]).astype(BF16)
    acc = jnp.zeros_like(x)
    for c in range(D_FF // FF_CHUNK):
        a = jnp.dot(h2, w1_ref[:, c * FF_CHUNK:(c + 1) * FF_CHUNK], preferred_element_type=F32)
        a = jnp.square(jnp.maximum(a, 0.0)).astype(BF16)
        acc = acc + jnp.dot(a, w2_ref[c * FF_CHUNK:(c + 1) * FF_CHUNK, :], preferred_element_type=F32)
    y = x + acc
    if final_norm:
        y = _rms(y, gfin_ref[...])
    y_ref[...] = y


def _out_ffn(x, od, dil_parts, gdil, wout, g2, w1, w2, gfin, final_norm, tm):
    t = x.shape[0]
    row = lambda w: pl.BlockSpec((tm, w), lambda i: (i, 0))
    const = lambda a: pl.BlockSpec(a.shape, lambda i: (0, 0), pipeline_mode=pl.Buffered(1))
    flat = [a for pair in dil_parts for a in pair]
    return pl.pallas_call(
        functools.partial(_out_ffn_kernel, final_norm=final_norm),
        grid=(t // tm,),
        in_specs=[row(D_MODEL), row(D_DIFF)] + [row(D_DIL)] * 6
                 + [const(gdil), const(wout), const(g2), const(w1), const(w2), const(gfin)],
        out_specs=row(D_MODEL),
        out_shape=jax.ShapeDtypeStruct((t, D_MODEL), F32),
        compiler_params=pltpu.CompilerParams(
            dimension_semantics=("arbitrary",), vmem_limit_bytes=VMEM_LIMIT),
        name="out_ffn",
    )(x, od, *flat, gdil, wout, g2, w1, w2, gfin)


def _rope_tables(batch, s_len):
    half = HEAD_DIM // 2
    inv_freq = ROPE_THETA ** (-jnp.arange(half, dtype=F32) / half)
    ang = jnp.arange(s_len).astype(F32)[:, None] * inv_freq[None, :]
    cos, sin = jnp.cos(ang), jnp.sin(ang)
    reps = LANES // HEAD_DIM
    cos = jnp.tile(cos, (batch, 2 * reps))
    sin = jnp.tile(jnp.concatenate([-sin, sin], axis=1), (batch, reps))
    return cos, sin


def _trunk(x, params, tm, tq, tk):
    (norm1_g, w_in, lq1, lk1, lq2, lk2, diff_norm_g, dil_norm_g, w_out, norm2_g, w_ff1, w_ff2,
     final_norm_g) = params
    batch, s_len, _ = x.shape
    depth = w_in.shape[0]
    cos, sin = _rope_tables(batch, s_len)
    row2 = lambda a: a.reshape(1, -1)
    y = x.reshape(batch * s_len, D_MODEL)
    for l in range(depth):
        lambda_init = 0.8 - 0.6 * math.exp(-0.3 * l)
        proj = _in_proj(y, row2(norm1_g[l]), w_in[l], cos, sin, min(tm, batch * s_len))
        od = _diff_attn(proj, row2(lq1[l]), row2(lk1[l]), row2(lq2[l]), row2(lk2[l]),
                        diff_norm_g[l].reshape(-1, 1), batch, s_len, lambda_init,
                        min(tq, s_len), min(tk, s_len))
        dil_parts = [_dilated(proj, dil, batch, s_len) for _, dil in DILATED_PATTERNS]
        y = _out_ffn(y, od, dil_parts, row2(dil_norm_g[l]), w_out[l], row2(norm2_g[l]),
                     w_ff1[l], w_ff2[l], row2(final_norm_g), l == depth - 1,
                     min(tm, batch * s_len))
    return y.reshape(x.shape)


def kernel(x_prompt, x_sample, norm1_g, w_in, lambda_q1, lambda_k1, lambda_q2, lambda_k2, diff_norm_g, dil_norm_g, w_out, norm2_g, w_ff1, w_ff2, final_norm_g):
    params = (norm1_g, w_in.astype(BF16), lambda_q1, lambda_k1, lambda_q2, lambda_k2, diff_norm_g,
              dil_norm_g, w_out.astype(BF16), norm2_g, w_ff1.astype(BF16), w_ff2.astype(BF16),
              final_norm_g)
    return (_trunk(x_prompt, params, tm=512, tq=512, tk=512),
            _trunk(x_sample, params, tm=512, tq=512, tk=512))
```

```python
import functools
import math

import jax
import jax.numpy as jnp
from jax import lax
from jax.experimental import pallas as pl
from jax.experimental.pallas import tpu as pltpu

D_MODEL = 1024
HEAD_DIM = 64
N_DIFF_HEADS = 4
DIFF_V_DIM = 2 * HEAD_DIM
D_DIFF = N_DIFF_HEADS * DIFF_V_DIM
N_DIL_HEADS = 8
D_DIL = N_DIL_HEADS * HEAD_DIM
D_IN = 3 * D_DIFF + 3 * D_DIL
D_FF = 4 * D_MODEL
DILATED_PATTERNS = ((128, 1), (512, 4), (2048, 16))
ROPE_THETA = 10000.0
NORM_EPS = 1e-5
NEG_INF = -1e30

LANES = 128
DQ_BLK, DK_BLK, DV_BLK = 0, D_DIFF // LANES, 2 * D_DIFF // LANES

VMEM_LIMIT = 56 * 1024 * 1024

BF16 = jnp.bfloat16
F32 = jnp.float32


def _rms(x, g):
    ms = jnp.mean(x * x, axis=-1, keepdims=True)
    return x * lax.rsqrt(ms + NORM_EPS) * g


IN_CHUNK = 512
DQ_SCALE = HEAD_DIM ** -0.5 * math.log2(math.e)
IN_CHUNK_KIND = ((True, DQ_SCALE), (True, 1.0), (False, 1.0),
                 (True, HEAD_DIM ** -0.5), (True, 1.0), (False, 1.0))


def _in_proj_kernel(x_ref, g_ref, w_ref, cos_ref, sin_ref, diff_ref, dil_ref):
    h = _rms(x_ref[...], g_ref[...]).astype(BF16)
    cos = cos_ref[...]
    sin = sin_ref[...]
    lane = lax.broadcasted_iota(jnp.int32, (1, LANES), 1)
    first_half = (lane % HEAD_DIM) < (HEAD_DIM // 2)
    n_diff_chunks = 3 * D_DIFF // IN_CHUNK

    def emit(c, lo, width, val):
        if c < n_diff_chunks:
            diff_ref[:, lo:lo + width] = val.astype(BF16)
        else:
            lo -= 3 * D_DIFF
            dil_ref[:, lo:lo + width] = val.astype(BF16).astype(F32)

    for c, (rope, scale) in enumerate(IN_CHUNK_KIND):
        acc = jnp.dot(h, w_ref[:, c * IN_CHUNK:(c + 1) * IN_CHUNK], preferred_element_type=F32)
        if not rope:
            emit(c, c * IN_CHUNK, IN_CHUNK, acc)
            continue
        for g in range(IN_CHUNK // LANES):
            xg = acc[:, g * LANES:(g + 1) * LANES]
            partner = jnp.where(first_half,
                                pltpu.roll(xg, LANES - HEAD_DIM // 2, axis=1),
                                pltpu.roll(xg, HEAD_DIM // 2, axis=1))
            r = xg * cos + partner * sin
            if scale != 1.0:
                r = r * scale
            emit(c, c * IN_CHUNK + g * LANES, LANES, r)


def _in_proj(x, g, w, cos, sin, tm):
    t = x.shape[0]
    return pl.pallas_call(
        _in_proj_kernel,
        grid=(t // tm,),
        in_specs=[
            pl.BlockSpec((tm, D_MODEL), lambda i: (i, 0)),
            pl.BlockSpec((1, D_MODEL), lambda i: (0, 0)),
            pl.BlockSpec((D_MODEL, D_IN), lambda i: (0, 0)),
            pl.BlockSpec((tm, LANES), lambda i: (i, 0)),
            pl.BlockSpec((tm, LANES), lambda i: (i, 0)),
        ],
        out_specs=[pl.BlockSpec((tm, 3 * D_DIFF), lambda i: (i, 0)),
                   pl.BlockSpec((tm, 3 * D_DIL), lambda i: (i, 0))],
        out_shape=[jax.ShapeDtypeStruct((t, 3 * D_DIFF), BF16),
                   jax.ShapeDtypeStruct((t, 3 * D_DIL), F32)],
        compiler_params=pltpu.CompilerParams(
            dimension_semantics=("arbitrary",), vmem_limit_bytes=VMEM_LIMIT),
        name="in_proj",
    )(x, g, w, cos, sin)


VT_ROWS = DIFF_V_DIM + 16
DIFF_UNROLL = 4


def _diff_attn_kernel(q_ref, k_ref, v_ref, lq1_ref, lk1_ref, lq2_ref, lk2_ref, g_ref, o_ref,
                      vt_ref, sa_ref, sb_ref, cma_ref, cmb_ref, pa_ref, pb_ref, ala_ref, alb_ref,
                      m_ref, acc_ref, *, lambda_init, unroll):
    n_chunks, _, tk = vt_ref.shape

    @pl.when(pl.program_id(2) == 0)
    def _():
        row = lax.broadcasted_iota(jnp.int32, (VT_ROWS - DIFF_V_DIM, tk), 0)
        ones_tile = jnp.where(row == 0, 1.0, 0.0).astype(BF16)

        def transpose_chunk(c, carry):
            start = pl.multiple_of(c * tk, tk)
            vt_ref[c, :DIFF_V_DIM, :] = v_ref[pl.ds(start, tk), :].astype(F32).T.astype(BF16)
            vt_ref[c, DIFF_V_DIM:, :] = ones_tile
            return carry
        lax.fori_loop(0, n_chunks, transpose_chunk, 0)

    q = q_ref[...]
    lane = lax.broadcasted_iota(jnp.int32, (1, LANES), 1)
    zero = jnp.zeros_like(q)
    qm = (jnp.where(lane < HEAD_DIM, q, zero), jnp.where(lane >= HEAD_DIM, q, zero))

    m_ref[...] = jnp.full(m_ref.shape, NEG_INF, F32)
    acc_ref[...] = jnp.zeros(acc_ref.shape, F32)

    s_bufs, cmax_bufs = (sa_ref, sb_ref), (cma_ref, cmb_ref)
    p_bufs, alpha_bufs = (pa_ref, pb_ref), (ala_ref, alb_ref)

    def scores(j, slot):
        start = pl.multiple_of(j * tk, tk)
        kc = k_ref[pl.ds(start, tk), :]
        for p in range(2):
            s = lax.dot_general(kc, qm[p], (((1,), (1,)), ((), ())),
                                preferred_element_type=F32)
            s_bufs[slot][p] = s
            cmax_bufs[slot][p] = jnp.max(s, axis=0, keepdims=True)

    def softmax(slot):
        for p in range(2):
            m_old = m_ref[p]
            m_new = jnp.maximum(m_old, cmax_bufs[slot][p])
            alpha_bufs[slot][p] = jnp.exp2(m_old - m_new)
            p_bufs[slot][p] = jnp.exp2(s_bufs[slot][p] - m_new).astype(BF16)
            m_ref[p] = m_new

    def pv(j, slot):
        vt = vt_ref[j]
        for p in range(2):
            acc_ref[p] = (alpha_bufs[slot][p] * acc_ref[p]
                          + jnp.dot(vt, p_bufs[slot][p], preferred_element_type=F32))

    scores(0, 0)
    pb_ref[...] = jnp.zeros(pb_ref.shape, BF16)
    alb_ref[...] = jnp.ones(alb_ref.shape, F32)

    def body(jj, carry):
        for u in range(unroll):
            j = unroll * jj + u
            cur, nxt = u % 2, 1 - u % 2
            pv(jnp.maximum(j - 1, 0), nxt)
            softmax(cur)
            scores(jnp.minimum(j + 1, n_chunks - 1), nxt)
        return carry

    lax.fori_loop(0, n_chunks // unroll, body, 0)
    pv(n_chunks - 1, (n_chunks - 1) % 2)

    lam = (jnp.exp(jnp.sum(lq1_ref[...] * lk1_ref[...], axis=-1, keepdims=True))
           - jnp.exp(jnp.sum(lq2_ref[...] * lk2_ref[...], axis=-1, keepdims=True))
           + lambda_init)
    o1 = acc_ref[0, :DIFF_V_DIM, :] / acc_ref[0, DIFF_V_DIM:DIFF_V_DIM + 1, :]
    o2 = acc_ref[1, :DIFF_V_DIM, :] / acc_ref[1, DIFF_V_DIM:DIFF_V_DIM + 1, :]
    o = o1 - lam * o2
    ms = jnp.mean(o * o, axis=0, keepdims=True)
    o = o * lax.rsqrt(ms + NORM_EPS) * g_ref[...] * (1.0 - lambda_init)
    o_ref[...] = o.T.astype(o_ref.dtype)


def _diff_attn(proj, lq1, lk1, lq2, lk2, g, batch, s_len, lambda_init, tq, tk):
    assert s_len % tq == 0 and s_len % (2 * tk) == 0
    nq = s_len // tq
    n_chunks = s_len // tk
    unroll = DIFF_UNROLL if n_chunks % DIFF_UNROLL == 0 else 2
    small = pl.BlockSpec((1, HEAD_DIM), lambda b, h, i: (0, 0))
    return pl.pallas_call(
        functools.partial(_diff_attn_kernel, lambda_init=lambda_init, unroll=unroll),
        grid=(batch, N_DIFF_HEADS, nq),
        in_specs=[
            pl.BlockSpec((tq, LANES), lambda b, h, i: (b * nq + i, DQ_BLK + h)),
            pl.BlockSpec((s_len, LANES), lambda b, h, i: (b, DK_BLK + h)),
            pl.BlockSpec((s_len, LANES), lambda b, h, i: (b, DV_BLK + h)),
            small, small, small, small,
            pl.BlockSpec((DIFF_V_DIM, 1), lambda b, h, i: (0, 0)),
        ],
        out_specs=pl.BlockSpec((tq, LANES), lambda b, h, i: (b * nq + i, h)),
        out_shape=jax.ShapeDtypeStruct((batch * s_len, D_DIFF), BF16),
        scratch_shapes=[
            pltpu.VMEM((s_len // tk, VT_ROWS, tk), BF16),
            pltpu.VMEM((2, tk, tq), F32),
            pltpu.VMEM((2, tk, tq), F32),
            pltpu.VMEM((2, 1, tq), F32),
            pltpu.VMEM((2, 1, tq), F32),
            pltpu.VMEM((2, tk, tq), BF16),
            pltpu.VMEM((2, tk, tq), BF16),
            pltpu.VMEM((2, 1, tq), F32),
            pltpu.VMEM((2, 1, tq), F32),
            pltpu.VMEM((2, 1, tq), F32),
            pltpu.VMEM((2, VT_ROWS, tq), F32),
        ],
        compiler_params=pltpu.CompilerParams(
            dimension_semantics=("arbitrary", "arbitrary", "arbitrary"),
            vmem_limit_bytes=VMEM_LIMIT),
        name="diff_attn",
    )(proj, proj, proj, lq1, lk1, lq2, lk2, g)


DIL_HALF = 64
DIL_TQ = 4096
DIL_CQ = 256


def _dilated_kernel(q_ref, k_ref, v_ref, o_ref, lse_ref, *, dilations):
    s_len = k_ref.shape[0]
    tq = q_ref.shape[0]
    pos0 = pl.program_id(2) * tq
    lane = lax.broadcasted_iota(jnp.int32, (1, LANES), 1)
    head0 = lane < HEAD_DIM

    for n_pat, d in enumerate(dilations):
        l_len = s_len // d
        nq = tq // d
        cq = min(DIL_CQ, nq)
        n_sub = nq // cq
        win = min(l_len, cq + 2 * DIL_HALF)
        offs = (lax.broadcasted_iota(jnp.int32, (cq, win), 0)
                - lax.broadcasted_iota(jnp.int32, (cq, win), 1))

        def rows(start, size, d=d):
            return pl.ds(start, size) if d == 1 else pl.ds(start, size, stride=d)

        def unit(u, carry, d=d, cq=cq, n_sub=n_sub, win=win, l_len=l_len, offs=offs,
                 rows=rows, first=(n_pat == 0)):
            r, t = u // n_sub, u % n_sub
            c0 = pos0 // d + t * cq
            ws = jnp.clip(c0 - DIL_HALF, 0, l_len - win)
            q_rows = rows(r + d * t * cq, cq)
            k_rows = rows(r + d * ws, win)
            q = q_ref[q_rows, :].astype(BF16)
            kw = k_ref[k_rows, :].astype(BF16)
            vw = v_ref[k_rows, :].astype(BF16)
            band = jnp.abs(offs + (c0 - ws)) <= DIL_HALF
            zero = jnp.zeros_like(q)
            outs, lses = [], []
            for in_head in (head0, jnp.logical_not(head0)):
                s = lax.dot_general(jnp.where(in_head, q, zero), kw, (((1,), (1,)), ((), ())),
                                    preferred_element_type=F32)
                s = jnp.where(band, s, NEG_INF)
                m = jnp.max(s, axis=-1, keepdims=True)
                e = jnp.exp(s - m)
                l = jnp.sum(e, axis=-1, keepdims=True)
                outs.append(jnp.dot(e.astype(BF16), vw, preferred_element_type=F32) / l)
                lses.append(m + jnp.log(l))
            o_new = jnp.where(head0, outs[0], outs[1])
            lse_new = jnp.where(head0, lses[0], lses[1])
            if not first:
                o_old, lse_old = o_ref[q_rows, :], lse_ref[q_rows, :]
                top = jnp.maximum(lse_old, lse_new)
                lse_sum = top + jnp.log(jnp.exp(lse_old - top) + jnp.exp(lse_new - top))
                o_new = o_old * jnp.exp(lse_old - lse_sum) + o_new * jnp.exp(lse_new - lse_sum)
                lse_new = lse_sum
            o_ref[q_rows, :] = o_new
            lse_ref[q_rows, :] = lse_new
            return carry

        lax.fori_loop(0, d * n_sub, unit, 0)


def _dilated(dil_proj, batch, s_len):
    tq = min(DIL_TQ, s_len)
    dilations = tuple(d for _, d in DILATED_PATTERNS)
    for window, d in DILATED_PATTERNS:
        assert window // 2 // d == DIL_HALF and tq % (d * 8) == 0 and s_len % tq == 0
        assert s_len // d >= min(DIL_CQ, tq // d) and (tq // d) % min(DIL_CQ, tq // d) == 0
    nt = s_len // tq
    n_pair = D_DIL // LANES
    whole = lambda blk: pl.BlockSpec((s_len, LANES), lambda b, hp, i: (b, blk + hp),
                                     pipeline_mode=pl.Buffered(1))
    return pl.pallas_call(
        functools.partial(_dilated_kernel, dilations=dilations),
        grid=(batch, n_pair, nt),
        in_specs=[
            pl.BlockSpec((tq, LANES), lambda b, hp, i: (b * nt + i, hp)),
            whole(n_pair),
            whole(2 * n_pair),
        ],
        out_specs=pl.BlockSpec((tq, LANES), lambda b, hp, i: (b * nt + i, hp)),
        out_shape=jax.ShapeDtypeStruct((batch * s_len, D_DIL), F32),
        scratch_shapes=[pltpu.VMEM((tq, LANES), F32)],
        compiler_params=pltpu.CompilerParams(
            dimension_semantics=("arbitrary",) * 3, vmem_limit_bytes=VMEM_LIMIT),
        name="dilated",
    )(dil_proj, dil_proj, dil_proj)


FF_CHUNK = 512


def _out_ffn_kernel(x_ref, od_ref, os_ref, gdil_ref, wout_ref, g2_ref, w1_ref, w2_ref, gfin_ref,
                    y_ref, *, final_norm):
    os_ = _rms(os_ref[...], gdil_ref[...]).astype(BF16)
    x = x_ref[...]
    x = x + jnp.dot(od_ref[...], wout_ref[:D_DIFF, :], preferred_element_type=F32)
    x = x + jnp.dot(os_, wout_ref[D_DIFF:, :], preferred_element_type=F32)
    h2 = _rms(x, g2_ref[...]).astype(BF16)
    acc = jnp.zeros_like(x)
    for c in range(D_FF // FF_CHUNK):
        a = jnp.dot(h2, w1_ref[:, c * FF_CHUNK:(c + 1) * FF_CHUNK], preferred_element_type=F32)
        a = jnp.square(jnp.maximum(a, 0.0)).astype(BF16)
        acc = acc + jnp.dot(a, w2_ref[c * FF_CHUNK:(c + 1) * FF_CHUNK, :], preferred_element_type=F32)
    y = x + acc
    if final_norm:
        y = _rms(y, gfin_ref[...])
    y_ref[...] = y


def _out_ffn(x, od, os_, gdil, wout, g2, w1, w2, gfin, final_norm, tm):
    t = x.shape[0]
    row = lambda w: pl.BlockSpec((tm, w), lambda i: (i, 0))
    const = lambda a: pl.BlockSpec(a.shape, lambda i: (0, 0), pipeline_mode=pl.Buffered(1))
    return pl.pallas_call(
        functools.partial(_out_ffn_kernel, final_norm=final_norm),
        grid=(t // tm,),
        in_specs=[row(D_MODEL), row(D_DIFF), row(D_DIL)]
                 + [const(gdil), const(wout), const(g2), const(w1), const(w2), const(gfin)],
        out_specs=row(D_MODEL),
        out_shape=jax.ShapeDtypeStruct((t, D_MODEL), F32),
        compiler_params=pltpu.CompilerParams(
            dimension_semantics=("arbitrary",), vmem_limit_bytes=VMEM_LIMIT),
        name="out_ffn",
    )(x, od, os_, gdil, wout, g2, w1, w2, gfin)


def _rope_tables(batch, s_len):
    half = HEAD_DIM // 2
    inv_freq = ROPE_THETA ** (-jnp.arange(half, dtype=F32) / half)
    ang = jnp.arange(s_len).astype(F32)[:, None] * inv_freq[None, :]
    cos, sin = jnp.cos(ang), jnp.sin(ang)
    reps = LANES // HEAD_DIM
    cos = jnp.tile(cos, (batch, 2 * reps))
    sin = jnp.tile(jnp.concatenate([-sin, sin], axis=1), (batch, reps))
    return cos, sin


def _trunk(x, params, tm, tq, tk):
    (norm1_g, w_in, lq1, lk1, lq2, lk2, diff_norm_g, dil_norm_g, w_out, norm2_g, w_ff1, w_ff2,
     final_norm_g) = params
    batch, s_len, _ = x.shape
    depth = w_in.shape[0]
    cos, sin = _rope_tables(batch, s_len)
    row2 = lambda a: a.reshape(1, -1)
    y = x.reshape(batch * s_len, D_MODEL)
    for l in range(depth):
        lambda_init = 0.8 - 0.6 * math.exp(-0.3 * l)
        proj, dil_proj = _in_proj(y, row2(norm1_g[l]), w_in[l], cos, sin, min(tm, batch * s_len))
        od = _diff_attn(proj, row2(lq1[l]), row2(lk1[l]), row2(lq2[l]), row2(lk2[l]),
                        diff_norm_g[l].reshape(-1, 1), batch, s_len, lambda_init,
                        min(tq, s_len), min(tk, s_len))
        os_ = _dilated(dil_proj, batch, s_len)
        y = _out_ffn(y, od, os_, row2(dil_norm_g[l]), w_out[l], row2(norm2_g[l]),
                     w_ff1[l], w_ff2[l], row2(final_norm_g), l == depth - 1,
                     min(tm, batch * s_len))
    return y.reshape(x.shape)


def kernel(x_prompt, x_sample, norm1_g, w_in, lambda_q1, lambda_k1, lambda_q2, lambda_k2, diff_norm_g, dil_norm_g, w_out, norm2_g, w_ff1, w_ff2, final_norm_g):
    params = (norm1_g, w_in.astype(BF16), lambda_q1, lambda_k1, lambda_q2, lambda_k2, diff_norm_g,
              dil_norm_g, w_out.astype(BF16), norm2_g, w_ff1.astype(BF16), w_ff2.astype(BF16),
              final_norm_g)
    return (_trunk(x_prompt, params, tm=512, tq=512, tk=512),
            _trunk(x_sample, params, tm=512, tq=512, tk=512))
```

```python
import functools
import math

import jax
import jax.numpy as jnp
from jax import lax
from jax.experimental import pallas as pl
from jax.experimental.pallas import tpu as pltpu

D_MODEL = 1024
HEAD_DIM = 64
N_DIFF_HEADS = 4
DIFF_V_DIM = 2 * HEAD_DIM
D_DIFF = N_DIFF_HEADS * DIFF_V_DIM
N_DIL_HEADS = 8
D_DIL = N_DIL_HEADS * HEAD_DIM
D_IN = 3 * D_DIFF + 3 * D_DIL
D_FF = 4 * D_MODEL
DILATED_PATTERNS = ((128, 1), (512, 4), (2048, 16))
ROPE_THETA = 10000.0
NORM_EPS = 1e-5
NEG_INF = -1e30

LANES = 128
DQ_BLK, DK_BLK, DV_BLK = 0, D_DIFF // LANES, 2 * D_DIFF // LANES

VMEM_LIMIT = 56 * 1024 * 1024

BF16 = jnp.bfloat16
F32 = jnp.float32
F8 = jnp.float8_e4m3fn


def _rms(x, g):
    ms = jnp.mean(x * x, axis=-1, keepdims=True)
    return x * lax.rsqrt(ms + NORM_EPS) * g


IN_CHUNK = 512
DQ_SCALE = HEAD_DIM ** -0.5 * math.log2(math.e)
IN_CHUNK_KIND = ((True, DQ_SCALE), (True, 1.0), (False, 1.0),
                 (True, DQ_SCALE), (True, 1.0), (False, 1.0))


RES16 = 16


def _in_proj_kernel(x_ref, g_ref, w_ref, cos_ref, sin_ref, diff_ref, dil_ref, dil16_ref,
                    stage_ref):
    h = _rms(x_ref[...], g_ref[...]).astype(BF16)
    cos = cos_ref[...]
    sin = sin_ref[...]
    lane = lax.broadcasted_iota(jnp.int32, (1, LANES), 1)
    first_half = (lane % HEAD_DIM) < (HEAD_DIM // 2)
    n_diff_chunks = 3 * D_DIFF // IN_CHUNK

    rows16 = dil_ref.shape[0] // RES16

    def emit(c, lo, width, val):
        if c < n_diff_chunks:
            diff_ref[:, lo:lo + width] = val.astype(BF16)
            return
        lo -= 3 * D_DIFF
        rounded = val.astype(BF16).astype(F32)
        dil_ref[:, lo:lo + width] = rounded
        for piece in range(width // LANES):
            g = lo // LANES + piece
            stage_ref[g] = rounded[:, piece * LANES:(piece + 1) * LANES]
            for r in range(RES16):
                dil16_ref[r, :, g * LANES:(g + 1) * LANES] = (
                    stage_ref[g, pl.ds(r, rows16, stride=RES16), :].astype(BF16))

    for c, (rope, scale) in enumerate(IN_CHUNK_KIND):
        acc = jnp.dot(h, w_ref[:, c * IN_CHUNK:(c + 1) * IN_CHUNK], preferred_element_type=F32)
        if not rope:
            emit(c, c * IN_CHUNK, IN_CHUNK, acc)
            continue
        for g in range(IN_CHUNK // LANES):
            xg = acc[:, g * LANES:(g + 1) * LANES]
            partner = jnp.where(first_half,
                                pltpu.roll(xg, LANES - HEAD_DIM // 2, axis=1),
                                pltpu.roll(xg, HEAD_DIM // 2, axis=1))
            r = xg * cos + partner * sin
            if scale != 1.0:
                r = r * scale
            emit(c, c * IN_CHUNK + g * LANES, LANES, r)


def _in_proj(x, g, w, layer, cos, sin, s_len, tm):
    t = x.shape[0]
    assert s_len % tm == 0 and cos.shape[0] >= s_len
    pos_block = lambda i: (i % (s_len // tm), 0)
    return pl.pallas_call(
        _in_proj_kernel,
        grid=(t // tm,),
        in_specs=[
            pl.BlockSpec((tm, D_MODEL), lambda i: (i, 0)),
            pl.BlockSpec((1, D_MODEL), lambda i: (0, 0)),
            pl.BlockSpec((None, D_MODEL, D_IN), lambda i: (layer, 0, 0)),
            pl.BlockSpec((tm, LANES), pos_block),
            pl.BlockSpec((tm, LANES), pos_block),
        ],
        out_specs=[pl.BlockSpec((tm, 3 * D_DIFF), lambda i: (i, 0)),
                   pl.BlockSpec((tm, 3 * D_DIL), lambda i: (i, 0)),
                   pl.BlockSpec((RES16, tm // RES16, 3 * D_DIL), lambda i: (0, i, 0))],
        out_shape=[jax.ShapeDtypeStruct((t, 3 * D_DIFF), BF16),
                   jax.ShapeDtypeStruct((t, 3 * D_DIL), F32),
                   jax.ShapeDtypeStruct((RES16, t // RES16, 3 * D_DIL), BF16)],
        scratch_shapes=[pltpu.VMEM((3 * D_DIL // LANES, tm, LANES), F32)],
        compiler_params=pltpu.CompilerParams(
            dimension_semantics=("arbitrary",), vmem_limit_bytes=VMEM_LIMIT),
        name="in_proj",
    )(x, g, w, cos, sin)


VT_ROWS = DIFF_V_DIM + 16
DIFF_UNROLL = 16


BOUND_SLACK = 1.03
MIN_DENOMINATOR = 2.0 ** -80
F8_PEAK = 192.0
F8_K = 4 * HEAD_DIM


def _f8_split(x):
    hi = x.astype(F8).astype(F32)
    return hi, x - hi


def _lane_max(x, first):
    return (jnp.max(jnp.where(first, x, 0.0), axis=1, keepdims=True),
            jnp.max(jnp.where(first, 0.0, x), axis=1, keepdims=True))


def _map_sq_norms(x):
    sub = lax.broadcasted_iota(jnp.int32, (8, LANES), 0)
    lane = lax.broadcasted_iota(jnp.int32, (8, LANES), 1)
    select = jnp.where(lane // HEAD_DIM == sub, 1.0, 0.0).astype(BF16)
    xf = x.astype(F32)
    return lax.dot_general(select, (xf * xf).astype(BF16), (((1,), (1,)), ((), ())),
                           preferred_element_type=F32)


def _diff_attn_kernel(q_ref, k_ref, v_ref, lq1_ref, lk1_ref, lq2_ref, lk2_ref, g_ref, o_ref,
                      vt_ref, k8_ref, kn_ref, ka_ref, sa_ref, sb_ref, cma_ref, cmb_ref, pa_ref,
                      pb_ref, ala_ref, alb_ref, m_ref, acc_ref, *, lambda_init, unroll):
    n_chunks, _, tk = vt_ref.shape
    lane = lax.broadcasted_iota(jnp.int32, (1, LANES), 1)
    first = lane < HEAD_DIM

    @pl.when(pl.program_id(2) == 0)
    def _():
        row = lax.broadcasted_iota(jnp.int32, (VT_ROWS - DIFF_V_DIM, tk), 0)
        ones_tile = jnp.where(row == 0, 1.0, 0.0).astype(BF16)

        def per_chunk(c, carry):
            kn2, amax = carry
            start = pl.multiple_of(c * tk, tk)
            vt_ref[c, :DIFF_V_DIM, :] = v_ref[pl.ds(start, tk), :].astype(F32).T.astype(BF16)
            vt_ref[c, DIFF_V_DIM:, :] = ones_tile
            kc = k_ref[pl.ds(start, tk), :]
            return (jnp.maximum(kn2, _map_sq_norms(kc)),
                    jnp.maximum(amax, jnp.abs(kc.astype(F32))))
        kn2, amax = lax.fori_loop(0, n_chunks, per_chunk,
                                  (jnp.zeros((8, tk), F32), jnp.zeros((tk, LANES), F32)))
        kn_ref[...] = jnp.broadcast_to(jnp.max(kn2, axis=1, keepdims=True), kn_ref.shape)
        a0, a1 = [F8_PEAK / jnp.maximum(a, 1e-30)
                  for a in _lane_max(jnp.max(amax, axis=0, keepdims=True), first)]
        k_scale = jnp.where(first, a0, a1)
        ka_ref[0:1, :] = k_scale
        ka_ref[1:2, :] = jnp.broadcast_to(a0, (1, LANES))
        ka_ref[2:3, :] = jnp.broadcast_to(a1, (1, LANES))

        def split_chunk(c, carry):
            start = pl.multiple_of(c * tk, tk)
            hi, lo = _f8_split(k_ref[pl.ds(start, tk), :].astype(F32) * k_scale)
            hi_sw = pltpu.roll(hi, HEAD_DIM, axis=1)
            lo_sw = pltpu.roll(lo, HEAD_DIM, axis=1)
            zero = jnp.zeros_like(hi)
            k8_ref[c, 0] = jnp.concatenate([jnp.where(first, hi, lo_sw),
                                            jnp.where(first, hi, zero)], axis=1).astype(F8)
            k8_ref[c, 1] = jnp.concatenate([jnp.where(first, hi_sw, lo),
                                            jnp.where(first, hi_sw, zero)], axis=1).astype(F8)
            return carry
        lax.fori_loop(0, n_chunks, split_chunk, 0)

    q = q_ref[...]

    qn2 = _map_sq_norms(q)
    shift = [jnp.sqrt(qn2[p:p + 1, :] * kn_ref[p:p + 1, :1]) * BOUND_SLACK for p in range(2)]
    acc_ref[...] = jnp.zeros(acc_ref.shape, F32)

    qf = q.astype(F32)
    q_peak = _lane_max(jnp.max(jnp.abs(qf), axis=0, keepdims=True), first)
    qa = [F8_PEAK / jnp.maximum(a, 1e-30) for a in q_peak]
    unscale = [1.0 / (qa[p] * ka_ref[1 + p:2 + p, :1]) for p in range(2)]
    hi, lo = _f8_split(qf * jnp.where(first, qa[0], qa[1]))
    hi_sw = pltpu.roll(hi, HEAD_DIM, axis=1)
    lo_sw = pltpu.roll(lo, HEAD_DIM, axis=1)
    zero = jnp.zeros_like(hi)
    q8 = (jnp.concatenate([jnp.where(first, hi, hi_sw), jnp.where(first, lo, zero)],
                          axis=1).astype(F8),
          jnp.concatenate([jnp.where(first, hi_sw, hi), jnp.where(first, lo_sw, zero)],
                          axis=1).astype(F8))

    s_bufs = (sa_ref, sb_ref)

    def scores(j, p, slot):
        s_bufs[slot][p] = lax.dot_general(k8_ref[j, p], q8[p], (((1,), (1,)), ((), ())),
                                          preferred_element_type=F32)

    scores(0, 0, 0)
    scores(0, 1, 0)

    def bounded_steps(jj, last_body):
        for u in range(unroll):
            j = unroll * jj + u
            cur, nxt = u % 2, 1 - u % 2
            vt = vt_ref[j]
            for p in range(2):
                if not (last_body and u == unroll - 1):
                    scores(j + 1, p, nxt)
                e = jnp.exp2(s_bufs[cur][p] * unscale[p] - shift[p])
                acc_ref[p] += jnp.dot(vt, e.astype(BF16), preferred_element_type=F32)

    def bounded_body(jj, carry):
        bounded_steps(jj, False)
        return carry

    n_bodies = n_chunks // unroll
    if n_bodies > 1:
        lax.fori_loop(0, n_bodies - 1, bounded_body, 0)
    bounded_steps(n_bodies - 1, True)
    den_min = jnp.min(jnp.minimum(acc_ref[0, DIFF_V_DIM:DIFF_V_DIM + 1, :],
                                  acc_ref[1, DIFF_V_DIM:DIFF_V_DIM + 1, :]))

    @pl.when(jnp.logical_not(den_min > MIN_DENOMINATOR))
    def _():
        _diff_attn_running_max(q_ref, k_ref, vt_ref, sa_ref, sb_ref, cma_ref, cmb_ref, pa_ref,
                               pb_ref, ala_ref, alb_ref, m_ref, acc_ref)

    lam = (jnp.exp(jnp.sum(lq1_ref[...] * lk1_ref[...], axis=-1, keepdims=True))
           - jnp.exp(jnp.sum(lq2_ref[...] * lk2_ref[...], axis=-1, keepdims=True))
           + lambda_init)
    o1 = acc_ref[0, :DIFF_V_DIM, :] / acc_ref[0, DIFF_V_DIM:DIFF_V_DIM + 1, :]
    o2 = acc_ref[1, :DIFF_V_DIM, :] / acc_ref[1, DIFF_V_DIM:DIFF_V_DIM + 1, :]
    o = o1 - lam * o2
    ms = jnp.mean(o * o, axis=0, keepdims=True)
    o = o * lax.rsqrt(ms + NORM_EPS) * g_ref[...] * (1.0 - lambda_init)
    o_ref[...] = o.T.astype(o_ref.dtype)


def _diff_attn_running_max(q_ref, k_ref, vt_ref, sa_ref, sb_ref, cma_ref, cmb_ref, pa_ref, pb_ref,
                           ala_ref, alb_ref, m_ref, acc_ref):
    n_chunks, _, tk = vt_ref.shape
    unroll = 2
    q = q_ref[...]
    lane = lax.broadcasted_iota(jnp.int32, (1, LANES), 1)
    zero = jnp.zeros_like(q)
    qm = (jnp.where(lane < HEAD_DIM, q, zero), jnp.where(lane >= HEAD_DIM, q, zero))

    m_ref[...] = jnp.full(m_ref.shape, NEG_INF, F32)
    acc_ref[...] = jnp.zeros(acc_ref.shape, F32)

    s_bufs, cmax_bufs = (sa_ref, sb_ref), (cma_ref, cmb_ref)
    p_bufs, alpha_bufs = (pa_ref, pb_ref), (ala_ref, alb_ref)

    def scores(j, slot):
        start = pl.multiple_of(j * tk, tk)
        kc = k_ref[pl.ds(start, tk), :]
        for p in range(2):
            s = lax.dot_general(kc, qm[p], (((1,), (1,)), ((), ())),
                                preferred_element_type=F32)
            s_bufs[slot][p] = s
            cmax_bufs[slot][p] = jnp.max(s, axis=0, keepdims=True)

    def softmax(slot):
        for p in range(2):
            m_old = m_ref[p]
            m_new = jnp.maximum(m_old, cmax_bufs[slot][p])
            alpha_bufs[slot][p] = jnp.exp2(m_old - m_new)
            p_bufs[slot][p] = jnp.exp2(s_bufs[slot][p] - m_new).astype(BF16)
            m_ref[p] = m_new

    def pv(j, slot):
        vt = vt_ref[j]
        for p in range(2):
            acc_ref[p] = (alpha_bufs[slot][p] * acc_ref[p]
                          + jnp.dot(vt, p_bufs[slot][p], preferred_element_type=F32))

    scores(0, 0)
    pb_ref[...] = jnp.zeros(pb_ref.shape, BF16)
    alb_ref[...] = jnp.ones(alb_ref.shape, F32)

    def body(jj, carry):
        for u in range(unroll):
            j = unroll * jj + u
            cur, nxt = u % 2, 1 - u % 2
            pv(jnp.maximum(j - 1, 0), nxt)
            softmax(cur)
            scores(jnp.minimum(j + 1, n_chunks - 1), nxt)
        return carry

    lax.fori_loop(0, n_chunks // unroll, body, 0)
    pv(n_chunks - 1, (n_chunks - 1) % 2)


def _diff_attn(proj, lq1, lk1, lq2, lk2, g, batch, s_len, lambda_init, tq, tk):
    assert s_len % tq == 0 and s_len % (2 * tk) == 0
    nq = s_len // tq
    n_chunks = s_len // tk
    unroll = DIFF_UNROLL if n_chunks % DIFF_UNROLL == 0 else 2
    small = pl.BlockSpec((1, HEAD_DIM), lambda b, h, i: (0, 0))
    return pl.pallas_call(
        functools.partial(_diff_attn_kernel, lambda_init=lambda_init, unroll=unroll),
        grid=(batch, N_DIFF_HEADS, nq),
        in_specs=[
            pl.BlockSpec((tq, LANES), lambda b, h, i: (b * nq + i, DQ_BLK + h)),
            pl.BlockSpec((s_len, LANES), lambda b, h, i: (b, DK_BLK + h)),
            pl.BlockSpec((s_len, LANES), lambda b, h, i: (b, DV_BLK + h)),
            small, small, small, small,
            pl.BlockSpec((DIFF_V_DIM, 1), lambda b, h, i: (0, 0)),
        ],
        out_specs=pl.BlockSpec((tq, LANES), lambda b, h, i: (b * nq + i, h)),
        out_shape=jax.ShapeDtypeStruct((batch * s_len, D_DIFF), BF16),
        scratch_shapes=[
            pltpu.VMEM((s_len // tk, VT_ROWS, tk), BF16),
            pltpu.VMEM((s_len // tk, 2, tk, F8_K), F8),
            pltpu.VMEM((8, LANES), F32),
            pltpu.VMEM((8, LANES), F32),
            pltpu.VMEM((2, tk, tq), F32),
            pltpu.VMEM((2, tk, tq), F32),
            pltpu.VMEM((2, 1, tq), F32),
            pltpu.VMEM((2, 1, tq), F32),
            pltpu.VMEM((2, tk, tq), BF16),
            pltpu.VMEM((2, tk, tq), BF16),
            pltpu.VMEM((2, 1, tq), F32),
            pltpu.VMEM((2, 1, tq), F32),
            pltpu.VMEM((2, 1, tq), F32),
            pltpu.VMEM((2, VT_ROWS, tq), F32),
        ],
        compiler_params=pltpu.CompilerParams(
            dimension_semantics=("arbitrary", "arbitrary", "arbitrary"),
            vmem_limit_bytes=VMEM_LIMIT),
        name="diff_attn",
    )(proj, proj, proj, lq1, lk1, lq2, lk2, g)


DIL_HALF = 64
DIL_TQ = 4096
DIL_CQ = 128
DIL_TEMP_BYTES = 8 * 1024 * 1024


def _dilated_kernel(q_ref, k_ref, v_ref, q16_ref, k16_ref, v16_ref, o_ref, lse_ref, s_ref,
                    bias_ref, *, dilations):
    s_len = k_ref.shape[0]
    tq = q_ref.shape[0]
    pos0 = pl.program_id(2) * tq
    lane = lax.broadcasted_iota(jnp.int32, (1, LANES), 1)
    head0 = lane < HEAD_DIM

    for n_pat, d in enumerate(dilations):
        l_len = s_len // d
        nq = tq // d
        cq = min(DIL_CQ, nq)
        n_sub = nq // cq
        win = min(l_len, cq + 2 * DIL_HALF)
        offs = (lax.broadcasted_iota(jnp.int32, (cq, win), 0)
                - lax.broadcasted_iota(jnp.int32, (cq, win), 1))

        def rows(start, size, d=d):
            return pl.ds(start, size) if d == 1 else pl.ds(start, size, stride=d)

        n_units = d * n_sub

        regrouped = d == RES16

        def place(u, d=d, cq=cq, n_sub=n_sub, win=win, l_len=l_len, rows=rows):
            r, t = u // n_sub, u % n_sub
            c0 = pos0 // d + t * cq
            ws = jnp.clip(c0 - DIL_HALF, 0, l_len - win)
            return r, t, ws, c0 - ws, rows(r + d * t * cq, cq), rows(r + d * ws, win)

        def keys(ref, ref16, r, ws, k_rows, regrouped=regrouped, win=win):
            if regrouped:
                return ref16[r, pl.ds(pl.multiple_of(ws, DIL_HALF), win), :]
            return ref[k_rows, :].astype(BF16)

        def scores(u, slot, place=place, keys=keys, regrouped=regrouped, cq=cq, win=win):
            r, t, ws, _, q_rows, k_rows = place(u)
            if regrouped:
                q = q16_ref[r, pl.ds(pl.multiple_of(t * cq, cq), cq), :]
            else:
                q = q_ref[q_rows, :].astype(BF16)
            kw = keys(k_ref, k16_ref, r, ws, k_rows)
            zero = jnp.zeros_like(q)
            for hh, in_head in enumerate((head0, jnp.logical_not(head0))):
                s_ref[slot, hh, :cq, :win] = lax.dot_general(
                    jnp.where(in_head, q, zero), kw, (((1,), (1,)), ((), ())),
                    preferred_element_type=F32)

        for kk in range(3):
            bias_ref[kk, :cq, :win] = jnp.where(jnp.abs(offs + kk * DIL_HALF) <= DIL_HALF,
                                                0.0, NEG_INF)

        def finish(u, slot, place=place, keys=keys, cq=cq, win=win, first=(n_pat == 0)):
            r, _, ws, shift, q_rows, k_rows = place(u)
            vw = keys(v_ref, v16_ref, r, ws, k_rows)
            bias = bias_ref[shift // DIL_HALF, :cq, :win]
            outs, lses = [], []
            for hh in range(2):
                s = s_ref[slot, hh, :cq, :win] + bias
                m = jnp.max(s, axis=-1, keepdims=True)
                e = jnp.exp2(s - m)
                l = jnp.sum(e, axis=-1, keepdims=True)
                outs.append(jnp.dot(e.astype(BF16), vw, preferred_element_type=F32) / l)
                lses.append(m + jnp.log2(l))
            o_new = jnp.where(head0, outs[0], outs[1])
            lse_new = jnp.where(head0, lses[0], lses[1])
            if not first:
                o_old, lse_old = o_ref[q_rows, :], lse_ref[q_rows, :]
                old_larger = lse_old >= lse_new
                t = jnp.exp2(-jnp.abs(lse_old - lse_new))
                w_larger = 1.0 / (1.0 + t)
                w_smaller = t * w_larger
                o_new = (o_old * jnp.where(old_larger, w_larger, w_smaller)
                         + o_new * jnp.where(old_larger, w_smaller, w_larger))
                lse_new = jnp.maximum(lse_old, lse_new) + jnp.log2(1.0 + t)
            o_ref[q_rows, :] = o_new
            lse_ref[q_rows, :] = lse_new

        scores(0, 0)
        scores(1, 1)

        def body(uu, carry, scores=scores, finish=finish, n_units=n_units):
            for half in range(2):
                u = 4 * uu + 2 * half
                rd, wr = 2 * half, 2 - 2 * half
                for k in range(2):
                    scores(jnp.minimum(u + 2 + k, n_units - 1), wr + k)
                for k in range(2):
                    finish(u + k, rd + k)
            return carry

        lax.fori_loop(0, n_units // 4, body, 0)


def _dilated(dil_proj, dil16, batch, s_len):
    tq = min(DIL_TQ, s_len)
    dilations = tuple(sorted((d for _, d in DILATED_PATTERNS), reverse=True))
    assert dilations[0] == RES16
    for window, d in DILATED_PATTERNS:
        assert window // 2 // d == DIL_HALF and tq % (d * 8) == 0 and s_len % tq == 0
        assert s_len // d >= min(DIL_CQ, tq // d) and (tq // d) % min(DIL_CQ, tq // d) == 0
        assert (tq // min(DIL_CQ, tq // d)) % 4 == 0
        assert s_len // d >= min(DIL_CQ, tq // d) + 2 * DIL_HALF or s_len // d == min(DIL_CQ, tq // d)
    nt = s_len // tq
    n_pair = D_DIL // LANES
    kv32_bytes, kv16_bytes = 2 * s_len * LANES * 4, 2 * s_len * LANES * 2
    tile_bytes = tq * LANES * (2 * 4 + 2 * 2 + 2 * 4 + 4)
    room = VMEM_LIMIT - tile_bytes - DIL_TEMP_BYTES
    both = 2 * (kv32_bytes + kv16_bytes) <= room
    small = kv32_bytes + 2 * kv16_bytes <= room
    whole = lambda blk: pl.BlockSpec((s_len, LANES), lambda b, hp, i: (b, blk + hp),
                                     pipeline_mode=pl.Buffered(2 if both else 1))
    whole16 = lambda blk: pl.BlockSpec((RES16, s_len // RES16, LANES),
                                       lambda b, hp, i: (0, b, blk + hp),
                                       pipeline_mode=pl.Buffered(2 if both or small else 1))
    return pl.pallas_call(
        functools.partial(_dilated_kernel, dilations=dilations),
        grid=(batch, n_pair, nt),
        in_specs=[
            pl.BlockSpec((tq, LANES), lambda b, hp, i: (b * nt + i, hp)),
            whole(n_pair),
            whole(2 * n_pair),
            pl.BlockSpec((RES16, tq // RES16, LANES), lambda b, hp, i: (0, b * nt + i, hp)),
            whole16(n_pair),
            whole16(2 * n_pair),
        ],
        out_specs=pl.BlockSpec((tq, LANES), lambda b, hp, i: (b * nt + i, hp)),
        out_shape=jax.ShapeDtypeStruct((batch * s_len, D_DIL), F32),
        scratch_shapes=[pltpu.VMEM((tq, LANES), F32),
                        pltpu.VMEM((4, 2, DIL_CQ, DIL_CQ + 2 * DIL_HALF), F32),
                        pltpu.VMEM((3, DIL_CQ, DIL_CQ + 2 * DIL_HALF), F32)],
        compiler_params=pltpu.CompilerParams(
            dimension_semantics=("arbitrary",) * 3, vmem_limit_bytes=VMEM_LIMIT),
        name="dilated",
    )(dil_proj, dil_proj, dil_proj, dil16, dil16, dil16)


FF_CHUNK = 512


def _out_ffn_kernel(x_ref, od_ref, os_ref, gdil_ref, wout_ref, g2_ref, w1_ref, w2_ref, gfin_ref,
                    y_ref, *, final_norm):
    os_ = _rms(os_ref[...], gdil_ref[...]).astype(BF16)
    x = x_ref[...]
    x = x + jnp.dot(od_ref[...], wout_ref[:D_DIFF, :], preferred_element_type=F32)
    x = x + jnp.dot(os_, wout_ref[D_DIFF:, :], preferred_element_type=F32)
    h2 = _rms(x, g2_ref[...]).astype(BF16)
    acc = jnp.zeros_like(x)
    for c in range(D_FF // FF_CHUNK):
        a = jnp.dot(h2, w1_ref[:, c * FF_CHUNK:(c + 1) * FF_CHUNK], preferred_element_type=F32)
        a = jnp.square(jnp.maximum(a, 0.0)).astype(BF16)
        acc = acc + jnp.dot(a, w2_ref[c * FF_CHUNK:(c + 1) * FF_CHUNK, :], preferred_element_type=F32)
    y = x + acc
    if final_norm:
        y = _rms(y, gfin_ref[...])
    y_ref[...] = y


def _out_ffn(x, od, os_, gdil, wout, g2, w1, w2, layer, gfin, final_norm, tm):
    t = x.shape[0]
    row = lambda w: pl.BlockSpec((tm, w), lambda i: (i, 0))
    const = lambda a: pl.BlockSpec(a.shape, lambda i: (0, 0), pipeline_mode=pl.Buffered(1))
    of_layer = lambda a: pl.BlockSpec((None,) + a.shape[1:], lambda i: (layer, 0, 0),
                                      pipeline_mode=pl.Buffered(1))
    return pl.pallas_call(
        functools.partial(_out_ffn_kernel, final_norm=final_norm),
        grid=(t // tm,),
        in_specs=[row(D_MODEL), row(D_DIFF), row(D_DIL)]
                 + [const(gdil), of_layer(wout), const(g2), of_layer(w1), of_layer(w2),
                    const(gfin)],
        out_specs=row(D_MODEL),
        out_shape=jax.ShapeDtypeStruct((t, D_MODEL), F32),
        compiler_params=pltpu.CompilerParams(
            dimension_semantics=("arbitrary",), vmem_limit_bytes=VMEM_LIMIT),
        name="out_ffn",
    )(x, od, os_, gdil, wout, g2, w1, w2, gfin)


def _rope_tables(n_pos):
    half = HEAD_DIM // 2
    lane = jnp.arange(LANES)
    inv_freq = ROPE_THETA ** (-(lane % half).astype(F32) / half)
    sign = jnp.where(lane % HEAD_DIM < half, -1.0, 1.0).astype(F32)
    ang = jnp.arange(n_pos).astype(F32)[:, None] * inv_freq[None, :]
    return jnp.cos(ang), jnp.sin(ang) * sign[None, :]


def _trunk(x, params, rope, tm, tq, tk):
    (norm1_g, w_in, lq1, lk1, lq2, lk2, diff_norm_g, dil_norm_g, w_out, norm2_g, w_ff1, w_ff2,
     final_norm_g) = params
    batch, s_len, _ = x.shape
    depth = w_in.shape[0]
    cos, sin = rope
    tm = min(tm, s_len)
    row2 = lambda a: a.reshape(1, -1)
    y = x.reshape(batch * s_len, D_MODEL)
    for l in range(depth):
        lambda_init = 0.8 - 0.6 * math.exp(-0.3 * l)
        proj, dil_proj, dil16 = _in_proj(y, row2(norm1_g[l]), w_in, l, cos, sin, s_len, tm)
        od = _diff_attn(proj, row2(lq1[l]), row2(lk1[l]), row2(lq2[l]), row2(lk2[l]),
                        diff_norm_g[l].reshape(-1, 1), batch, s_len, lambda_init,
                        min(tq, s_len), min(tk, s_len))
        os_ = _dilated(dil_proj, dil16, batch, s_len)
        y = _out_ffn(y, od, os_, row2(dil_norm_g[l]), w_out, row2(norm2_g[l]),
                     w_ff1, w_ff2, l, row2(final_norm_g), l == depth - 1, tm)
    return y.reshape(x.shape)


def kernel(x_prompt, x_sample, norm1_g, w_in, lambda_q1, lambda_k1, lambda_q2, lambda_k2, diff_norm_g, dil_norm_g, w_out, norm2_g, w_ff1, w_ff2, final_norm_g):
    params = (norm1_g, w_in.astype(BF16), lambda_q1, lambda_k1, lambda_q2, lambda_k2, diff_norm_g,
              dil_norm_g, w_out.astype(BF16), norm2_g, w_ff1.astype(BF16), w_ff2.astype(BF16),
              final_norm_g)
    rope = _rope_tables(max(x_prompt.shape[1], x_sample.shape[1]))
    return (_trunk(x_prompt, params, rope, tm=512, tq=512, tk=512),
            _trunk(x_sample, params, rope, tm=512, tq=512, tk=512))
```

```python
import functools
import math

import jax
import jax.numpy as jnp
from jax import lax
from jax.experimental import pallas as pl
from jax.experimental.pallas import tpu as pltpu

D_MODEL = 1024
HEAD_DIM = 64
N_DIFF_HEADS = 4
DIFF_V_DIM = 2 * HEAD_DIM
D_DIFF = N_DIFF_HEADS * DIFF_V_DIM
N_DIL_HEADS = 8
D_DIL = N_DIL_HEADS * HEAD_DIM
D_IN = 3 * D_DIFF + 3 * D_DIL
D_FF = 4 * D_MODEL
DILATED_PATTERNS = ((128, 1), (512, 4), (2048, 16))
ROPE_THETA = 10000.0
NORM_EPS = 1e-5
NEG_INF = -1e30

LANES = 128
DQ_BLK, DK_BLK, DV_BLK = 0, D_DIFF // LANES, 2 * D_DIFF // LANES

VMEM_LIMIT = 56 * 1024 * 1024

BF16 = jnp.bfloat16
F32 = jnp.float32
F8 = jnp.float8_e4m3fn


def _rms(x, g):
    ms = jnp.mean(x * x, axis=-1, keepdims=True)
    return x * lax.rsqrt(ms + NORM_EPS) * g


IN_CHUNK = 512
DQ_SCALE = HEAD_DIM ** -0.5 * math.log2(math.e)
IN_CHUNK_KIND = ((True, DQ_SCALE), (True, 1.0), (False, 1.0),
                 (True, DQ_SCALE), (True, 1.0), (False, 1.0))


RES16 = 16


def _in_proj_kernel(x_ref, g_ref, w_ref, cos_ref, sin_ref, diff_ref, dil_ref, dil16_ref):
    h = _rms(x_ref[...], g_ref[...]).astype(BF16)
    cos = cos_ref[...]
    sin = sin_ref[...]
    lane = lax.broadcasted_iota(jnp.int32, (1, LANES), 1)
    first_half = (lane % HEAD_DIM) < (HEAD_DIM // 2)
    n_diff_chunks = 3 * D_DIFF // IN_CHUNK

    rows16 = dil_ref.shape[1] // RES16

    def emit(c, lo, width, val):
        if c < n_diff_chunks:
            diff_ref[:, lo:lo + width] = val.astype(BF16)
            return
        lo -= 3 * D_DIFF
        rounded = val.astype(BF16).astype(F32)
        for piece in range(width // LANES):
            g = lo // LANES + piece
            dil_ref[g] = rounded[:, piece * LANES:(piece + 1) * LANES]
            for r in range(RES16):
                dil16_ref[r, :, g * LANES:(g + 1) * LANES] = (
                    dil_ref[g, pl.ds(r, rows16, stride=RES16), :].astype(BF16))

    for c, (rope, scale) in enumerate(IN_CHUNK_KIND):
        acc = jnp.dot(h, w_ref[:, c * IN_CHUNK:(c + 1) * IN_CHUNK], preferred_element_type=F32)
        if not rope:
            emit(c, c * IN_CHUNK, IN_CHUNK, acc)
            continue
        for g in range(IN_CHUNK // LANES):
            xg = acc[:, g * LANES:(g + 1) * LANES]
            partner = jnp.where(first_half,
                                pltpu.roll(xg, LANES - HEAD_DIM // 2, axis=1),
                                pltpu.roll(xg, HEAD_DIM // 2, axis=1))
            r = xg * cos + partner * sin
            if scale != 1.0:
                r = r * scale
            emit(c, c * IN_CHUNK + g * LANES, LANES, r)


def _in_proj(x, g, w, layer, cos, sin, s_len, tm):
    t = x.shape[0]
    assert s_len % tm == 0 and cos.shape[0] >= s_len
    pos_block = lambda i: (i % (s_len // tm), 0)
    return pl.pallas_call(
        _in_proj_kernel,
        grid=(t // tm,),
        in_specs=[
            pl.BlockSpec((tm, D_MODEL), lambda i: (i, 0)),
            pl.BlockSpec((1, D_MODEL), lambda i: (0, 0)),
            pl.BlockSpec((None, D_MODEL, D_IN), lambda i: (layer, 0, 0)),
            pl.BlockSpec((tm, LANES), pos_block),
            pl.BlockSpec((tm, LANES), pos_block),
        ],
        out_specs=[pl.BlockSpec((tm, 3 * D_DIFF), lambda i: (i, 0)),
                   pl.BlockSpec((3 * D_DIL // LANES, tm, LANES), lambda i: (0, i, 0)),
                   pl.BlockSpec((RES16, tm // RES16, 3 * D_DIL), lambda i: (0, i, 0))],
        out_shape=[jax.ShapeDtypeStruct((t, 3 * D_DIFF), BF16),
                   jax.ShapeDtypeStruct((3 * D_DIL // LANES, t, LANES), F32),
                   jax.ShapeDtypeStruct((RES16, t // RES16, 3 * D_DIL), BF16)],
        compiler_params=pltpu.CompilerParams(
            dimension_semantics=("arbitrary",), vmem_limit_bytes=VMEM_LIMIT),
        name="in_proj",
    )(x, g, w, cos, sin)


VT_ROWS = DIFF_V_DIM + 16
DIFF_UNROLL = 16


BOUND_SLACK = 1.03
MIN_DENOMINATOR = 2.0 ** -80
F8_PEAK = 192.0
F8_K = 4 * HEAD_DIM


def _f8_split(x):
    hi = x.astype(F8).astype(F32)
    return hi, x - hi


def _lane_max(x, first):
    return (jnp.max(jnp.where(first, x, 0.0), axis=1, keepdims=True),
            jnp.max(jnp.where(first, 0.0, x), axis=1, keepdims=True))


def _map_sq_norms(x):
    sub = lax.broadcasted_iota(jnp.int32, (8, LANES), 0)
    lane = lax.broadcasted_iota(jnp.int32, (8, LANES), 1)
    select = jnp.where(lane // HEAD_DIM == sub, 1.0, 0.0).astype(BF16)
    xf = x.astype(F32)
    return lax.dot_general(select, (xf * xf).astype(BF16), (((1,), (1,)), ((), ())),
                           preferred_element_type=F32)


def _diff_attn_kernel(q_ref, k_ref, v_ref, lq1_ref, lk1_ref, lq2_ref, lk2_ref, g_ref, o_ref,
                      vt_ref, k8_ref, kn_ref, ka_ref, sa_ref, sb_ref, cma_ref, cmb_ref, pa_ref,
                      pb_ref, ala_ref, alb_ref, m_ref, acc_ref, *, lambda_init, unroll):
    n_chunks, _, tk = vt_ref.shape
    lane = lax.broadcasted_iota(jnp.int32, (1, LANES), 1)
    first = lane < HEAD_DIM

    @pl.when(pl.program_id(2) == 0)
    def _():
        row = lax.broadcasted_iota(jnp.int32, (VT_ROWS - DIFF_V_DIM, tk), 0)
        ones_tile = jnp.where(row == 0, 1.0, 0.0).astype(BF16)

        def per_chunk(c, carry):
            kn2, amax = carry
            start = pl.multiple_of(c * tk, tk)
            vt_ref[c, :DIFF_V_DIM, :] = v_ref[pl.ds(start, tk), :].astype(F32).T.astype(BF16)
            vt_ref[c, DIFF_V_DIM:, :] = ones_tile
            kc = k_ref[pl.ds(start, tk), :]
            return (jnp.maximum(kn2, _map_sq_norms(kc)),
                    jnp.maximum(amax, jnp.abs(kc.astype(F32))))
        kn2, amax = lax.fori_loop(0, n_chunks, per_chunk,
                                  (jnp.zeros((8, tk), F32), jnp.zeros((tk, LANES), F32)))
        kn_ref[...] = jnp.broadcast_to(jnp.max(kn2, axis=1, keepdims=True), kn_ref.shape)
        a0, a1 = [F8_PEAK / jnp.maximum(a, 1e-30)
                  for a in _lane_max(jnp.max(amax, axis=0, keepdims=True), first)]
        k_scale = jnp.where(first, a0, a1)
        ka_ref[0:1, :] = k_scale
        ka_ref[1:2, :] = jnp.broadcast_to(a0, (1, LANES))
        ka_ref[2:3, :] = jnp.broadcast_to(a1, (1, LANES))

        def split_chunk(c, carry):
            start = pl.multiple_of(c * tk, tk)
            hi, lo = _f8_split(k_ref[pl.ds(start, tk), :].astype(F32) * k_scale)
            hi_sw = pltpu.roll(hi, HEAD_DIM, axis=1)
            lo_sw = pltpu.roll(lo, HEAD_DIM, axis=1)
            zero = jnp.zeros_like(hi)
            k8_ref[c, 0] = jnp.concatenate([jnp.where(first, hi, lo_sw),
                                            jnp.where(first, hi, zero)], axis=1).astype(F8)
            k8_ref[c, 1] = jnp.concatenate([jnp.where(first, hi_sw, lo),
                                            jnp.where(first, hi_sw, zero)], axis=1).astype(F8)
            return carry
        lax.fori_loop(0, n_chunks, split_chunk, 0)

    q = q_ref[...]

    qn2 = _map_sq_norms(q)
    shift = [jnp.sqrt(qn2[p:p + 1, :] * kn_ref[p:p + 1, :1]) * BOUND_SLACK for p in range(2)]
    acc_ref[...] = jnp.zeros(acc_ref.shape, F32)

    qf = q.astype(F32)
    q_peak = _lane_max(jnp.max(jnp.abs(qf), axis=0, keepdims=True), first)
    qa = [F8_PEAK / jnp.maximum(a, 1e-30) for a in q_peak]
    unscale = [1.0 / (qa[p] * ka_ref[1 + p:2 + p, :1]) for p in range(2)]
    hi, lo = _f8_split(qf * jnp.where(first, qa[0], qa[1]))
    hi_sw = pltpu.roll(hi, HEAD_DIM, axis=1)
    lo_sw = pltpu.roll(lo, HEAD_DIM, axis=1)
    zero = jnp.zeros_like(hi)
    q8 = (jnp.concatenate([jnp.where(first, hi, hi_sw), jnp.where(first, lo, zero)],
                          axis=1).astype(F8),
          jnp.concatenate([jnp.where(first, hi_sw, hi), jnp.where(first, lo_sw, zero)],
                          axis=1).astype(F8))

    s_bufs = (sa_ref, sb_ref)

    def scores(j, p, slot):
        s_bufs[slot][p] = lax.dot_general(k8_ref[j, p], q8[p], (((1,), (1,)), ((), ())),
                                          preferred_element_type=F32)

    scores(0, 0, 0)
    scores(0, 1, 0)

    def bounded_steps(jj, last_body):
        for u in range(unroll):
            j = unroll * jj + u
            cur, nxt = u % 2, 1 - u % 2
            vt = vt_ref[j]
            for p in range(2):
                if not (last_body and u == unroll - 1):
                    scores(j + 1, p, nxt)
                e = jnp.exp2(s_bufs[cur][p] * unscale[p] - shift[p])
                acc_ref[p] += jnp.dot(vt, e.astype(BF16), preferred_element_type=F32)

    def bounded_body(jj, carry):
        bounded_steps(jj, False)
        return carry

    n_bodies = n_chunks // unroll
    if n_bodies > 1:
        lax.fori_loop(0, n_bodies - 1, bounded_body, 0)
    bounded_steps(n_bodies - 1, True)
    den_min = jnp.min(jnp.minimum(acc_ref[0, DIFF_V_DIM:DIFF_V_DIM + 1, :],
                                  acc_ref[1, DIFF_V_DIM:DIFF_V_DIM + 1, :]))

    @pl.when(jnp.logical_not(den_min > MIN_DENOMINATOR))
    def _():
        _diff_attn_running_max(q_ref, k_ref, vt_ref, sa_ref, sb_ref, cma_ref, cmb_ref, pa_ref,
                               pb_ref, ala_ref, alb_ref, m_ref, acc_ref)

    lam = (jnp.exp(jnp.sum(lq1_ref[...] * lk1_ref[...], axis=-1, keepdims=True))
           - jnp.exp(jnp.sum(lq2_ref[...] * lk2_ref[...], axis=-1, keepdims=True))
           + lambda_init)
    o1 = acc_ref[0, :DIFF_V_DIM, :] / acc_ref[0, DIFF_V_DIM:DIFF_V_DIM + 1, :]
    o2 = acc_ref[1, :DIFF_V_DIM, :] / acc_ref[1, DIFF_V_DIM:DIFF_V_DIM + 1, :]
    o = o1 - lam * o2
    ms = jnp.mean(o * o, axis=0, keepdims=True)
    o = o * lax.rsqrt(ms + NORM_EPS) * g_ref[...] * (1.0 - lambda_init)
    o_ref[...] = o.T.astype(o_ref.dtype)


def _diff_attn_running_max(q_ref, k_ref, vt_ref, sa_ref, sb_ref, cma_ref, cmb_ref, pa_ref, pb_ref,
                           ala_ref, alb_ref, m_ref, acc_ref):
    n_chunks, _, tk = vt_ref.shape
    unroll = 2
    q = q_ref[...]
    lane = lax.broadcasted_iota(jnp.int32, (1, LANES), 1)
    zero = jnp.zeros_like(q)
    qm = (jnp.where(lane < HEAD_DIM, q, zero), jnp.where(lane >= HEAD_DIM, q, zero))

    m_ref[...] = jnp.full(m_ref.shape, NEG_INF, F32)
    acc_ref[...] = jnp.zeros(acc_ref.shape, F32)

    s_bufs, cmax_bufs = (sa_ref, sb_ref), (cma_ref, cmb_ref)
    p_bufs, alpha_bufs = (pa_ref, pb_ref), (ala_ref, alb_ref)

    def scores(j, slot):
        start = pl.multiple_of(j * tk, tk)
        kc = k_ref[pl.ds(start, tk), :]
        for p in range(2):
            s = lax.dot_general(kc, qm[p], (((1,), (1,)), ((), ())),
                                preferred_element_type=F32)
            s_bufs[slot][p] = s
            cmax_bufs[slot][p] = jnp.max(s, axis=0, keepdims=True)

    def softmax(slot):
        for p in range(2):
            m_old = m_ref[p]
            m_new = jnp.maximum(m_old, cmax_bufs[slot][p])
            alpha_bufs[slot][p] = jnp.exp2(m_old - m_new)
            p_bufs[slot][p] = jnp.exp2(s_bufs[slot][p] - m_new).astype(BF16)
            m_ref[p] = m_new

    def pv(j, slot):
        vt = vt_ref[j]
        for p in range(2):
            acc_ref[p] = (alpha_bufs[slot][p] * acc_ref[p]
                          + jnp.dot(vt, p_bufs[slot][p], preferred_element_type=F32))

    scores(0, 0)
    pb_ref[...] = jnp.zeros(pb_ref.shape, BF16)
    alb_ref[...] = jnp.ones(alb_ref.shape, F32)

    def body(jj, carry):
        for u in range(unroll):
            j = unroll * jj + u
            cur, nxt = u % 2, 1 - u % 2
            pv(jnp.maximum(j - 1, 0), nxt)
            softmax(cur)
            scores(jnp.minimum(j + 1, n_chunks - 1), nxt)
        return carry

    lax.fori_loop(0, n_chunks // unroll, body, 0)
    pv(n_chunks - 1, (n_chunks - 1) % 2)


def _diff_attn(proj, lq1, lk1, lq2, lk2, g, batch, s_len, lambda_init, tq, tk):
    assert s_len % tq == 0 and s_len % (2 * tk) == 0
    nq = s_len // tq
    n_chunks = s_len // tk
    unroll = DIFF_UNROLL if n_chunks % DIFF_UNROLL == 0 else 2
    small = pl.BlockSpec((1, HEAD_DIM), lambda b, h, i: (0, 0))
    return pl.pallas_call(
        functools.partial(_diff_attn_kernel, lambda_init=lambda_init, unroll=unroll),
        grid=(batch, N_DIFF_HEADS, nq),
        in_specs=[
            pl.BlockSpec((tq, LANES), lambda b, h, i: (b * nq + i, DQ_BLK + h)),
            pl.BlockSpec((s_len, LANES), lambda b, h, i: (b, DK_BLK + h)),
            pl.BlockSpec((s_len, LANES), lambda b, h, i: (b, DV_BLK + h)),
            small, small, small, small,
            pl.BlockSpec((DIFF_V_DIM, 1), lambda b, h, i: (0, 0)),
        ],
        out_specs=pl.BlockSpec((tq, LANES), lambda b, h, i: (b * nq + i, h)),
        out_shape=jax.ShapeDtypeStruct((batch * s_len, D_DIFF), BF16),
        scratch_shapes=[
            pltpu.VMEM((s_len // tk, VT_ROWS, tk), BF16),
            pltpu.VMEM((s_len // tk, 2, tk, F8_K), F8),
            pltpu.VMEM((8, LANES), F32),
            pltpu.VMEM((8, LANES), F32),
            pltpu.VMEM((2, tk, tq), F32),
            pltpu.VMEM((2, tk, tq), F32),
            pltpu.VMEM((2, 1, tq), F32),
            pltpu.VMEM((2, 1, tq), F32),
            pltpu.VMEM((2, tk, tq), BF16),
            pltpu.VMEM((2, tk, tq), BF16),
            pltpu.VMEM((2, 1, tq), F32),
            pltpu.VMEM((2, 1, tq), F32),
            pltpu.VMEM((2, 1, tq), F32),
            pltpu.VMEM((2, VT_ROWS, tq), F32),
        ],
        compiler_params=pltpu.CompilerParams(
            dimension_semantics=("arbitrary", "arbitrary", "arbitrary"),
            vmem_limit_bytes=VMEM_LIMIT),
        name="diff_attn",
    )(proj, proj, proj, lq1, lk1, lq2, lk2, g)


DIL_HALF = 64
DIL_TQ = 4096
DIL_CQ = 128
DIL_TEMP_BYTES = 8 * 1024 * 1024


def _dilated_kernel(q_ref, k_ref, v_ref, q16_ref, k16_ref, v16_ref, o_ref, lse_ref, s_ref,
                    bias_ref, *, dilations):
    s_len = k_ref.shape[0]
    tq = q_ref.shape[0]
    pos0 = pl.program_id(2) * tq
    lane = lax.broadcasted_iota(jnp.int32, (1, LANES), 1)
    head0 = lane < HEAD_DIM

    for n_pat, d in enumerate(dilations):
        l_len = s_len // d
        nq = tq // d
        cq = min(DIL_CQ, nq)
        n_sub = nq // cq
        win = min(l_len, cq + 2 * DIL_HALF)
        offs = (lax.broadcasted_iota(jnp.int32, (cq, win), 0)
                - lax.broadcasted_iota(jnp.int32, (cq, win), 1))

        def rows(start, size, d=d):
            return pl.ds(start, size) if d == 1 else pl.ds(start, size, stride=d)

        n_units = d * n_sub

        regrouped = d == RES16

        def place(u, d=d, cq=cq, n_sub=n_sub, win=win, l_len=l_len, rows=rows):
            r, t = u // n_sub, u % n_sub
            c0 = pos0 // d + t * cq
            ws = jnp.clip(c0 - DIL_HALF, 0, l_len - win)
            return r, t, ws, c0 - ws, rows(r + d * t * cq, cq), rows(r + d * ws, win)

        def keys(ref, ref16, r, ws, k_rows, regrouped=regrouped, win=win):
            if regrouped:
                return ref16[r, pl.ds(pl.multiple_of(ws, DIL_HALF), win), :]
            return ref[k_rows, :].astype(BF16)

        def scores(u, slot, place=place, keys=keys, regrouped=regrouped, cq=cq, win=win):
            r, t, ws, _, q_rows, k_rows = place(u)
            if regrouped:
                q = q16_ref[r, pl.ds(pl.multiple_of(t * cq, cq), cq), :]
            else:
                q = q_ref[q_rows, :].astype(BF16)
            kw = keys(k_ref, k16_ref, r, ws, k_rows)
            zero = jnp.zeros_like(q)
            for hh, in_head in enumerate((head0, jnp.logical_not(head0))):
                s_ref[slot, hh, :cq, :win] = lax.dot_general(
                    jnp.where(in_head, q, zero), kw, (((1,), (1,)), ((), ())),
                    preferred_element_type=F32)

        for kk in range(3):
            bias_ref[kk, :cq, :win] = jnp.where(jnp.abs(offs + kk * DIL_HALF) <= DIL_HALF,
                                                0.0, NEG_INF)

        def finish(u, slot, place=place, keys=keys, cq=cq, win=win, first=(n_pat == 0)):
            r, _, ws, shift, q_rows, k_rows = place(u)
            vw = keys(v_ref, v16_ref, r, ws, k_rows)
            bias = bias_ref[shift // DIL_HALF, :cq, :win]
            outs, lses = [], []
            for hh in range(2):
                s = s_ref[slot, hh, :cq, :win] + bias
                m = jnp.max(s, axis=-1, keepdims=True)
                e = jnp.exp2(s - m)
                l = jnp.sum(e, axis=-1, keepdims=True)
                outs.append(jnp.dot(e.astype(BF16), vw, preferred_element_type=F32) / l)
                lses.append(m + jnp.log2(l))
            o_new = jnp.where(head0, outs[0], outs[1])
            lse_new = jnp.where(head0, lses[0], lses[1])
            if not first:
                o_old, lse_old = o_ref[q_rows, :], lse_ref[q_rows, :]
                old_larger = lse_old >= lse_new
                t = jnp.exp2(-jnp.abs(lse_old - lse_new))
                w_larger = 1.0 / (1.0 + t)
                w_smaller = t * w_larger
                o_new = (o_old * jnp.where(old_larger, w_larger, w_smaller)
                         + o_new * jnp.where(old_larger, w_smaller, w_larger))
                lse_new = jnp.maximum(lse_old, lse_new) + jnp.log2(1.0 + t)
            o_ref[q_rows, :] = o_new
            lse_ref[q_rows, :] = lse_new

        scores(0, 0)
        scores(1, 1)

        def body(uu, carry, scores=scores, finish=finish, n_units=n_units):
            for half in range(2):
                u = 4 * uu + 2 * half
                rd, wr = 2 * half, 2 - 2 * half
                for k in range(2):
                    scores(jnp.minimum(u + 2 + k, n_units - 1), wr + k)
                for k in range(2):
                    finish(u + k, rd + k)
            return carry

        lax.fori_loop(0, n_units // 4, body, 0)


def _dilated(dil_proj, dil16, batch, s_len):
    tq = min(DIL_TQ, s_len)
    dilations = tuple(sorted((d for _, d in DILATED_PATTERNS), reverse=True))
    assert dilations[0] == RES16
    for window, d in DILATED_PATTERNS:
        assert window // 2 // d == DIL_HALF and tq % (d * 8) == 0 and s_len % tq == 0
        assert s_len // d >= min(DIL_CQ, tq // d) and (tq // d) % min(DIL_CQ, tq // d) == 0
        assert (tq // min(DIL_CQ, tq // d)) % 4 == 0
        assert s_len // d >= min(DIL_CQ, tq // d) + 2 * DIL_HALF or s_len // d == min(DIL_CQ, tq // d)
    nt = s_len // tq
    n_pair = D_DIL // LANES
    kv32_bytes, kv16_bytes = 2 * s_len * LANES * 4, 2 * s_len * LANES * 2
    tile_bytes = tq * LANES * (2 * 4 + 2 * 2 + 2 * 4 + 4)
    room = VMEM_LIMIT - tile_bytes - DIL_TEMP_BYTES
    both = 2 * (kv32_bytes + kv16_bytes) <= room
    small = kv32_bytes + 2 * kv16_bytes <= room
    whole = lambda blk: pl.BlockSpec((None, s_len, LANES), lambda b, hp, i: (blk + hp, b, 0),
                                     pipeline_mode=pl.Buffered(2 if both else 1))
    whole16 = lambda blk: pl.BlockSpec((RES16, s_len // RES16, LANES),
                                       lambda b, hp, i: (0, b, blk + hp),
                                       pipeline_mode=pl.Buffered(2 if both or small else 1))
    return pl.pallas_call(
        functools.partial(_dilated_kernel, dilations=dilations),
        grid=(batch, n_pair, nt),
        in_specs=[
            pl.BlockSpec((None, tq, LANES), lambda b, hp, i: (hp, b * nt + i, 0)),
            whole(n_pair),
            whole(2 * n_pair),
            pl.BlockSpec((RES16, tq // RES16, LANES), lambda b, hp, i: (0, b * nt + i, hp)),
            whole16(n_pair),
            whole16(2 * n_pair),
        ],
        out_specs=pl.BlockSpec((tq, LANES), lambda b, hp, i: (b * nt + i, hp)),
        out_shape=jax.ShapeDtypeStruct((batch * s_len, D_DIL), F32),
        scratch_shapes=[pltpu.VMEM((tq, LANES), F32),
                        pltpu.VMEM((4, 2, DIL_CQ, DIL_CQ + 2 * DIL_HALF), F32),
                        pltpu.VMEM((3, DIL_CQ, DIL_CQ + 2 * DIL_HALF), F32)],
        compiler_params=pltpu.CompilerParams(
            dimension_semantics=("arbitrary",) * 3, vmem_limit_bytes=VMEM_LIMIT),
        name="dilated",
    )(dil_proj, dil_proj, dil_proj, dil16, dil16, dil16)


FF_CHUNK = 512


def _out_ffn_kernel(x_ref, od_ref, os_ref, gdil_ref, wout_ref, g2_ref, w1_ref, w2_ref, gfin_ref,
                    y_ref, *, final_norm):
    os_ = _rms(os_ref[...], gdil_ref[...]).astype(BF16)
    x = x_ref[...]
    x = x + jnp.dot(od_ref[...], wout_ref[:D_DIFF, :], preferred_element_type=F32)
    x = x + jnp.dot(os_, wout_ref[D_DIFF:, :], preferred_element_type=F32)
    h2 = _rms(x, g2_ref[...]).astype(BF16)
    acc = jnp.zeros_like(x)
    for c in range(D_FF // FF_CHUNK):
        a = jnp.dot(h2, w1_ref[:, c * FF_CHUNK:(c + 1) * FF_CHUNK], preferred_element_type=F32)
        a = jnp.square(jnp.maximum(a, 0.0)).astype(BF16)
        acc = acc + jnp.dot(a, w2_ref[c * FF_CHUNK:(c + 1) * FF_CHUNK, :], preferred_element_type=F32)
    y = x + acc
    if final_norm:
        y = _rms(y, gfin_ref[...])
    y_ref[...] = y


def _out_ffn(x, od, os_, gdil, wout, g2, w1, w2, layer, gfin, final_norm, tm):
    t = x.shape[0]
    row = lambda w: pl.BlockSpec((tm, w), lambda i: (i, 0))
    const = lambda a: pl.BlockSpec(a.shape, lambda i: (0, 0), pipeline_mode=pl.Buffered(1))
    of_layer = lambda a: pl.BlockSpec((None,) + a.shape[1:], lambda i: (layer, 0, 0),
                                      pipeline_mode=pl.Buffered(1))
    return pl.pallas_call(
        functools.partial(_out_ffn_kernel, final_norm=final_norm),
        grid=(t // tm,),
        in_specs=[row(D_MODEL), row(D_DIFF), row(D_DIL)]
                 + [const(gdil), of_layer(wout), const(g2), of_layer(w1), of_layer(w2),
                    const(gfin)],
        out_specs=row(D_MODEL),
        out_shape=jax.ShapeDtypeStruct((t, D_MODEL), F32),
        compiler_params=pltpu.CompilerParams(
            dimension_semantics=("arbitrary",), vmem_limit_bytes=VMEM_LIMIT),
        name="out_ffn",
    )(x, od, os_, gdil, wout, g2, w1, w2, gfin)


def _rope_tables(n_pos):
    half = HEAD_DIM // 2
    lane = jnp.arange(LANES)
    inv_freq = ROPE_THETA ** (-(lane % half).astype(F32) / half)
    sign = jnp.where(lane % HEAD_DIM < half, -1.0, 1.0).astype(F32)
    ang = jnp.arange(n_pos).astype(F32)[:, None] * inv_freq[None, :]
    return jnp.cos(ang), jnp.sin(ang) * sign[None, :]


def _trunk(x, params, rope, tm, tq, tk):
    (norm1_g, w_in, lq1, lk1, lq2, lk2, diff_norm_g, dil_norm_g, w_out, norm2_g, w_ff1, w_ff2,
     final_norm_g) = params
    batch, s_len, _ = x.shape
    depth = w_in.shape[0]
    cos, sin = rope
    tm = min(tm, s_len)
    row2 = lambda a: a.reshape(1, -1)
    y = x.reshape(batch * s_len, D_MODEL)
    for l in range(depth):
        lambda_init = 0.8 - 0.6 * math.exp(-0.3 * l)
        proj, dil_proj, dil16 = _in_proj(y, row2(norm1_g[l]), w_in, l, cos, sin, s_len, tm)
        od = _diff_attn(proj, row2(lq1[l]), row2(lk1[l]), row2(lq2[l]), row2(lk2[l]),
                        diff_norm_g[l].reshape(-1, 1), batch, s_len, lambda_init,
                        min(tq, s_len), min(tk, s_len))
        os_ = _dilated(dil_proj, dil16, batch, s_len)
        y = _out_ffn(y, od, os_, row2(dil_norm_g[l]), w_out, row2(norm2_g[l]),
                     w_ff1, w_ff2, l, row2(final_norm_g), l == depth - 1, tm)
    return y.reshape(x.shape)


def kernel(x_prompt, x_sample, norm1_g, w_in, lambda_q1, lambda_k1, lambda_q2, lambda_k2, diff_norm_g, dil_norm_g, w_out, norm2_g, w_ff1, w_ff2, final_norm_g):
    params = (norm1_g, w_in.astype(BF16), lambda_q1, lambda_k1, lambda_q2, lambda_k2, diff_norm_g,
              dil_norm_g, w_out.astype(BF16), norm2_g, w_ff1.astype(BF16), w_ff2.astype(BF16),
              final_norm_g)
    rope = _rope_tables(max(x_prompt.shape[1], x_sample.shape[1]))
    return (_trunk(x_prompt, params, rope, tm=512, tq=512, tk=512),
            _trunk(x_sample, params, rope, tm=512, tq=512, tk=512))
```
